```python
import math
import jax, jax.numpy as jnp
from jax import lax
import numpy as np

D_MODEL = 1024
BATCH = 8
SEQ = 2048
DEPTH = 4

CHUNK = 64
N_MIXERS = 4
ALPHA = (2.0 * DEPTH) ** 0.25
BETA = (8.0 * DEPTH) ** -0.25
LN_EPS = 1e-5

CONV_A_WIDTH = 3

RWKV_HEAD = 64
RWKV_HEADS = D_MODEL // RWKV_HEAD
RWKV_DECAY_LORA = 64
RWKV_AAA_LORA = 64
RWKV_GATE_LORA = 160
RWKV_GN_EPS = 64e-5
RWKV_N_MIX = 6

D_RNN = 1280
LRU_BLOCKS = 10
LRU_BW = D_RNN // LRU_BLOCKS
CONV_C_WIDTH = 4
LRU_C = 8.0

RET_HEADS = 4
RET_DK = D_MODEL // RET_HEADS
RET_DV = 2 * D_MODEL // RET_HEADS
ROPE_BASE = 10000.0
RET_GN_EPS = 1e-6

D_FF = 2816
N_EXPERTS = 8
TOP_K = 2
D_FF_EXPERT = 3584
N_DENSE = (DEPTH + 1) // 2
N_MOE = DEPTH // 2

kernel_name = "hybrid_streaming_encoder_trunk"


def layer_norm(x, g, b, eps=LN_EPS):
    xf = x.astype(jnp.float32)
    mu = jnp.mean(xf, -1, keepdims=True)
    var = jnp.mean(jnp.square(xf - mu), -1, keepdims=True)
    return ((xf - mu) * lax.rsqrt(var + eps) * g + b).astype(x.dtype)


def group_norm_heads(y, g, b, eps):
    yf = y.astype(jnp.float32)
    mu = jnp.mean(yf, -1, keepdims=True)
    var = jnp.mean(jnp.square(yf - mu), -1, keepdims=True)
    return (yf - mu) * lax.rsqrt(var + eps) * g + b


def head_rms_norm(y, eps):
    yf = y.astype(jnp.float32)
    return yf * lax.rsqrt(jnp.mean(jnp.square(yf), -1, keepdims=True) + eps)


def causal_dwconv(x, w):
    K = w.shape[0]
    S = x.shape[1]
    xp = jnp.pad(x, ((0, 0), (K - 1, 0), (0, 0)))
    return sum(xp[:, k:k + S] * w[k] for k in range(K))


def token_shift(x):
    return jnp.pad(x, ((0, 0), (1, 0), (0, 0)))[:, :-1]


def rotary(t, positions):
    half = t.shape[-1] // 2
    freq = ROPE_BASE ** -jnp.linspace(0.0, 1.0, half, dtype=jnp.float32)
    ang = positions.astype(jnp.float32)[..., None] * freq
    cos = jnp.cos(ang)[:, :, None, :]
    sin = jnp.sin(ang)[:, :, None, :]
    tf = t.astype(jnp.float32)
    t1, t2 = tf[..., :half], tf[..., half:]
    return jnp.concatenate([t1 * cos - t2 * sin, t1 * sin + t2 * cos], axis=-1)


def linear_recurrence(a, b):
    def combine(left, right):
        a_l, b_l = left
        a_r, b_r = right
        return a_l * a_r, a_r * b_l + b_r
    _, h = lax.associative_scan(combine, (a, b), axis=1)
    return h


def short_conv_mixer(x, w_in, conv_w, w_out):
    b_gate, c_gate, v = jnp.split(x @ w_in, 3, axis=-1)
    y = b_gate * causal_dwconv(c_gate * v, conv_w)
    return y @ w_out


def rwkv7_scan(r, w, k, v, a, b):
    Bsz, S, H, N = r.shape

    def step(state, inp):
        r_t, w_t, k_t, v_t, a_t, b_t = inp
        sa = jnp.einsum('bhvk,bhk->bhv', state, a_t)
        state = (state * w_t[:, :, None, :]
                 + sa[..., None] * b_t[:, :, None, :]
                 + v_t[..., None] * k_t[:, :, None, :])
        return state, jnp.einsum('bhvk,bhk->bhv', state, r_t)

    xs = tuple(jnp.moveaxis(t, 1, 0) for t in (r, w, k, v, a, b))
    s0 = jnp.zeros((Bsz, H, N, N), jnp.float32)
    _, y = lax.scan(step, s0, xs)
    return jnp.moveaxis(y, 0, 1)


def rwkv7_mixer(x, mix, w_r, w_k, w_v, w0, w1, w2, a0, a1, a2, g1, g2,
                k_k, k_a, r_k, gn_g, gn_b, w_o):
    Bsz, S, D = x.shape
    H, N = RWKV_HEADS, RWKV_HEAD
    xx = token_shift(x) - x
    xr, xw, xk, xv, xa, xg = (x + xx * mix[j] for j in range(RWKV_N_MIX))

    r = xr @ w_r
    w_log = -jax.nn.softplus(-(w0 + jnp.tanh(xw @ w1) @ w2)) - 0.5
    k = xk @ w_k
    v = xv @ w_v
    a = jax.nn.sigmoid(a0 + (xa @ a1) @ a2)
    g = jax.nn.sigmoid(xg @ g1) @ g2

    heads = lambda t: t.astype(jnp.float32).reshape(Bsz, S, H, N)
    kk = heads(k * k_k)
    kk = kk / jnp.maximum(jnp.sqrt(jnp.sum(jnp.square(kk), -1, keepdims=True)), 1e-12)
    k = k * (1 + (a - 1) * k_a)

    rh, kh, vh, ah = heads(r), heads(k), heads(v), heads(a)
    decay = jnp.exp(-jnp.exp(heads(w_log)))
    y = rwkv7_scan(rh, decay, kh, vh, -kk, kk * ah)

    y = group_norm_heads(y, gn_g.reshape(H, N), gn_b.reshape(H, N), RWKV_GN_EPS)
    bonus = jnp.sum(rh * kh * r_k, -1, keepdims=True) * vh
    y = (y + bonus).reshape(Bsz, S, D).astype(x.dtype)
    return (y * g) @ w_o


def rglru_mixer(x, positions, w_in, conv_w, conv_b, w_ga, b_ga, w_gx, b_gx, lam, w_out):
    Bsz, S, _ = x.shape
    gate, u = jnp.split(x @ w_in, 2, axis=-1)
    u = (causal_dwconv(u, conv_w) + conv_b).astype(jnp.float32)
    ub = u.reshape(Bsz, S, LRU_BLOCKS, LRU_BW)
    r = jax.nn.sigmoid(jnp.einsum('bsgi,gij->bsgj', ub, w_ga).reshape(Bsz, S, D_RNN) + b_ga)
    i = jax.nn.sigmoid(jnp.einsum('bsgi,gij->bsgj', ub, w_gx).reshape(Bsz, S, D_RNN) + b_gx)
    log_a = -LRU_C * r * jax.nn.softplus(-lam)
    reset = (positions == 0)[..., None]
    a = jnp.where(reset, 0.0, jnp.exp(log_a))
    mult = jnp.where(reset, 1.0, jnp.sqrt(-jnp.expm1(2.0 * log_a)))
    h = linear_recurrence(a, mult * (i * u))
    y = jax.nn.gelu(gate) * h.astype(x.dtype)
    return y @ w_out


def retention_mixer(x, positions, w_in, w_o):
    Bsz, S, D = x.shape
    H, DK, DV = RET_HEADS, RET_DK, RET_DV
    NC = S // CHUNK
    q, k, v, g = jnp.split(x @ w_in, [D, 2 * D, 4 * D], axis=-1)
    q = rotary(q.reshape(Bsz, S, H, DK), positions)
    k = rotary(k.reshape(Bsz, S, H, DK), positions) * (DK ** -0.5)
    v = v.astype(jnp.float32).reshape(Bsz, S, H, DV)

    def to_chunks(t):
        return t.reshape(Bsz, NC, CHUNK, H, t.shape[-1]).transpose(1, 0, 3, 2, 4)

    log_g = jnp.log1p(-(2.0 ** (-5.0 - jnp.arange(H, dtype=jnp.float32))))
    idx = jnp.arange(CHUNK, dtype=jnp.float32)
    intra = jnp.exp(jnp.abs(idx[:, None] - idx[None, :]) * log_g[:, None, None])
    q_dec = jnp.exp((idx + 1.0) * log_g[:, None])[..., None]
    k_dec = jnp.exp((CHUNK - 1.0 - idx) * log_g[:, None])[..., None]
    c_dec = jnp.exp(CHUNK * log_g)[:, None, None]

    def step(state, chunk):
        qc, kc, vc = chunk
        scores = jnp.einsum('bhid,bhjd->bhij', qc, kc) * intra
        out = (jnp.einsum('bhij,bhjv->bhiv', scores, vc)
               + jnp.einsum('bhid,bhdv->bhiv', qc * q_dec, state))
        state = state * c_dec + jnp.einsum('bhjd,bhjv->bhdv', kc * k_dec, vc)
        return state, out

    s0 = jnp.zeros((Bsz, H, DK, DV), jnp.float32)
    _, y = lax.scan(step, s0, (to_chunks(q), to_chunks(k), to_chunks(v)))
    y = y.transpose(1, 0, 3, 2, 4).reshape(Bsz, S, H, DV)
    y = head_rms_norm(y, RET_GN_EPS).reshape(Bsz, S, 2 * D).astype(x.dtype)
    return (jax.nn.silu(g) * y) @ w_o


def swiglu(x, w_gu, w_down):
    gt, up = jnp.split(x @ w_gu, 2, axis=-1)
    return (jax.nn.silu(gt) * up) @ w_down


def moe_ffn(x, w_router, w_gu, w_down):
    Bsz, S, D = x.shape
    xt = x.reshape(-1, D)
    logits = (xt @ w_router).astype(jnp.float32)
    top_v, top_i = lax.top_k(logits, TOP_K)
    probs = jax.nn.softmax(top_v, axis=-1)
    gates = jnp.sum(jax.nn.one_hot(top_i, N_EXPERTS, dtype=jnp.float32) * probs[..., None], axis=1)
    gates = gates.astype(x.dtype)
    out = jnp.zeros_like(xt)
    for e in range(N_EXPERTS):
        out = out + gates[:, e:e + 1] * swiglu(xt, w_gu[e], w_down[e])
    return out.reshape(Bsz, S, D)


def setup_inputs(seed: int = 0) -> dict:
    key = jax.random.key(seed)
    keys = jax.random.split(key, 64)
    counter = [0]

    def nxt():
        k = keys[counter[0]]
        counter[0] += 1
        return k

    def nrm(shape, scale):
        return jax.random.normal(nxt(), shape, jnp.float32) * scale

    def unif(shape, lo, hi):
        return jax.random.uniform(nxt(), shape, jnp.float32, lo, hi)

    D = D_MODEL
    inp = {}
    inp["x"] = nrm((BATCH, SEQ, D), 1.0)
    offsets = jax.random.randint(nxt(), (BATCH, 1), 0, 2, dtype=jnp.int32) * SEQ
    inp["positions"] = offsets + jnp.arange(SEQ, dtype=jnp.int32)[None, :]
    inp["ln_mix_g"] = 1.0 + nrm((DEPTH, D), 0.02)
    inp["ln_mix_b"] = nrm((DEPTH, D), 0.02)
    inp["ln_ffn_g"] = 1.0 + nrm((DEPTH, D), 0.02)
    inp["ln_ffn_b"] = nrm((DEPTH, D), 0.02)
    inp["a_w_in"] = nrm((D, 3 * D), D ** -0.5)
    inp["a_conv_w"] = nrm((CONV_A_WIDTH, D), 0.5)
    inp["a_w_out"] = nrm((D, D), BETA * D ** -0.5)
    inp["b_mix"] = unif((RWKV_N_MIX, D), 0.0, 1.0)
    inp["b_w_r"] = nrm((D, D), D ** -0.5)
    inp["b_w_k"] = nrm((D, D), D ** -0.5)
    inp["b_w_v"] = nrm((D, D), D ** -0.5)
    inp["b_w0"] = unif((D,), -6.0, -1.0)
    inp["b_w1"] = nrm((D, RWKV_DECAY_LORA), D ** -0.5)
    inp["b_w2"] = nrm((RWKV_DECAY_LORA, D), 0.5 * RWKV_DECAY_LORA ** -0.5)
    inp["b_a0"] = nrm((D,), 0.5)
    inp["b_a1"] = nrm((D, RWKV_AAA_LORA), D ** -0.5)
    inp["b_a2"] = nrm((RWKV_AAA_LORA, D), RWKV_AAA_LORA ** -0.5)
    inp["b_g1"] = nrm((D, RWKV_GATE_LORA), D ** -0.5)
    inp["b_g2"] = nrm((RWKV_GATE_LORA, D), RWKV_GATE_LORA ** -0.5)
    inp["b_k_k"] = 0.85 + nrm((D,), 0.05)
    inp["b_k_a"] = 1.0 + nrm((D,), 0.05)
    inp["b_r_k"] = nrm((RWKV_HEADS, RWKV_HEAD), 0.1)
    inp["b_gn_g"] = 1.0 + nrm((D,), 0.02)
    inp["b_gn_b"] = nrm((D,), 0.02)
    inp["b_w_o"] = nrm((D, D), BETA * D ** -0.5)
    inp["c_w_in"] = nrm((D, 2 * D_RNN), D ** -0.5)
    inp["c_conv_w"] = nrm((CONV_C_WIDTH, D_RNN), 0.5)
    inp["c_conv_b"] = nrm((D_RNN,), 0.02)
    inp["c_w_ga"] = nrm((LRU_BLOCKS, LRU_BW, LRU_BW), LRU_BW ** -0.5)
    inp["c_b_ga"] = nrm((D_RNN,), 0.02)
    inp["c_w_gx"] = nrm((LRU_BLOCKS, LRU_BW, LRU_BW), LRU_BW ** -0.5)
    inp["c_b_gx"] = nrm((D_RNN,), 0.02)
    a_pow = unif((D_RNN,), 0.9, 0.999) ** (1.0 / LRU_C)
    inp["c_lam"] = jnp.log(a_pow) - jnp.log1p(-a_pow)
    inp["c_w_out"] = nrm((D_RNN, D), BETA * D_RNN ** -0.5)
    inp["d_w_in"] = nrm((D, 6 * D), D ** -0.5)
    inp["d_w_o"] = nrm((2 * D, D), BETA * (2 * D) ** -0.5)
    inp["ffn_w_gu"] = nrm((N_DENSE, D, 2 * D_FF), D ** -0.5)
    inp["ffn_w_down"] = nrm((N_DENSE, D_FF, D), BETA * D_FF ** -0.5)
    inp["moe_w_router"] = nrm((N_MOE, D, N_EXPERTS), D ** -0.5)
    inp["moe_w_gu"] = nrm((N_MOE, N_EXPERTS, D, 2 * D_FF_EXPERT), D ** -0.5)
    inp["moe_w_down"] = nrm((N_MOE, N_EXPERTS, D_FF_EXPERT, D), BETA * D_FF_EXPERT ** -0.5)
    return inp


def reference(x, positions, ln_mix_g, ln_mix_b, ln_ffn_g, ln_ffn_b,
              a_w_in, a_conv_w, a_w_out,
              b_mix, b_w_r, b_w_k, b_w_v, b_w0, b_w1, b_w2, b_a0, b_a1, b_a2,
              b_g1, b_g2, b_k_k, b_k_a, b_r_k, b_gn_g, b_gn_b, b_w_o,
              c_w_in, c_conv_w, c_conv_b, c_w_ga, c_b_ga, c_w_gx, c_b_gx, c_lam, c_w_out,
              d_w_in, d_w_o,
              ffn_w_gu, ffn_w_down, moe_w_router, moe_w_gu, moe_w_down):
    mixers = (
        lambda h: short_conv_mixer(h, a_w_in, a_conv_w, a_w_out),
        lambda h: rwkv7_mixer(h, b_mix, b_w_r, b_w_k, b_w_v, b_w0, b_w1, b_w2, b_a0, b_a1, b_a2,
                              b_g1, b_g2, b_k_k, b_k_a, b_r_k, b_gn_g, b_gn_b, b_w_o),
        lambda h: rglru_mixer(h, positions, c_w_in, c_conv_w, c_conv_b, c_w_ga, c_b_ga,
                              c_w_gx, c_b_gx, c_lam, c_w_out),
        lambda h: retention_mixer(h, positions, d_w_in, d_w_o),
    )
    for i in range(DEPTH):
        x = layer_norm(ALPHA * x + mixers[i % N_MIXERS](x), ln_mix_g[i], ln_mix_b[i])
        j = i // 2
        if i % 2 == 0:
            f = swiglu(x, ffn_w_gu[j], ffn_w_down[j])
        else:
            f = moe_ffn(x, moe_w_router[j], moe_w_gu[j], moe_w_down[j])
        x = layer_norm(ALPHA * x + f, ln_ffn_g[i], ln_ffn_b[i])
    return x
```

```python
import functools
import math

import jax
import jax.numpy as jnp
from jax import lax
from jax.experimental import pallas as pl
from jax.experimental.pallas import tpu as pltpu

F32 = jnp.float32
BF16 = jnp.bfloat16

D_MODEL = 1024
DEPTH = 4
ALPHA = (2.0 * DEPTH) ** 0.25
LN_EPS = 1e-5

RWKV_HEAD = 64
RWKV_HEADS = D_MODEL // RWKV_HEAD
RWKV_GN_EPS = 64e-5
RWKV_CHUNK = 64

D_RNN = 1280
LRU_BLOCKS = 10
LRU_BW = D_RNN // LRU_BLOCKS
LRU_C = 8.0

RET_HEADS = 4
RET_DK = D_MODEL // RET_HEADS
RET_DV = 2 * D_MODEL // RET_HEADS
RET_CHUNK = 64
RET_SUPER = 256
ROPE_BASE = 10000.0
RET_GN_EPS = 1e-6

N_EXPERTS = 8
TOP_K = 2
MOE_TILE = 512

VMEM_LIMIT = 56 * 1024 * 1024


def _cparams(n_grid):
    return pltpu.CompilerParams(dimension_semantics=("arbitrary",) * n_grid,
                                vmem_limit_bytes=VMEM_LIMIT)


def _sigmoid(x):
    return 1.0 / (1.0 + jnp.exp(-x))


def _silu(x):
    return x * _sigmoid(x)


def _expm1(z):
    u = jnp.exp(z)
    safe_u = jnp.where(u == 1.0, 0.5, jnp.maximum(u, 0.5))
    small = (u - 1.0) * z / jnp.log(safe_u)
    return jnp.where(u == 1.0, z, jnp.where(u < 0.5, u - 1.0, small))


def _shift_rows(x, d, fill=0.0):
    rows = lax.broadcasted_iota(jnp.int32, x.shape, 0)
    return jnp.where(rows >= d, pltpu.roll(x, d, axis=0), fill)


def _mm(x, ws, epi, outs, *, tm, tn, name, rows=(), cols=()):
    M, K = x.shape
    n_j = outs[0][0] // tn
    n_i = M // tm
    assert M % tm == 0 and all(o[0] == n_j * tn for o in outs)
    nw = len(ws)

    in_specs = [pl.BlockSpec((tm, K), lambda j, i: (i, 0))]
    args = [x]
    for w, lead, off in ws:
        nl = len(lead)
        in_specs.append(pl.BlockSpec((None,) * nl + (K, tn),
                                     lambda j, i, lead=lead, off=off: tuple(lead) + (0, off + j)))
        args.append(w)
    for a, width, col_fn in rows:
        in_specs.append(pl.BlockSpec((tm, width), lambda j, i, col_fn=col_fn: (i, col_fn(j))))
        args.append(a)
    for a, off in cols:
        in_specs.append(pl.BlockSpec((a.shape[0], tn), lambda j, i, off=off: (0, off + j)))
        args.append(a)

    def kern(*refs):
        x_ref = refs[0]
        w_refs = refs[1:1 + nw]
        row_refs = refs[1 + nw:1 + nw + len(rows)]
        col_refs = refs[1 + nw + len(rows):1 + nw + len(rows) + len(cols)]
        out_refs = refs[1 + nw + len(rows) + len(cols):-nw]
        wb_refs = refs[-nw:]

        @pl.when(pl.program_id(1) == 0)
        def _():
            for w_ref, wb in zip(w_refs, wb_refs):
                wb[...] = w_ref[...].astype(BF16)

        xv = x_ref[...]
        accs = [jnp.dot(xv, wb[...], preferred_element_type=F32) for wb in wb_refs]
        res = epi(accs, [r[...] for r in row_refs], [c[...] for c in col_refs], pl.program_id(0))
        for o, r in zip(out_refs, res):
            o[...] = r.astype(o.dtype)

    out = pl.pallas_call(
        kern,
        grid=(n_j, n_i),
        in_specs=in_specs,
        out_specs=[pl.BlockSpec((tm, tn), lambda j, i: (i, j)) for _ in outs],
        out_shape=[jax.ShapeDtypeStruct((M, n), dt) for n, dt in outs],
        scratch_shapes=[pltpu.VMEM((K, tn), BF16) for _ in ws],
        compiler_params=_cparams(2),
        name=name,
    )(*args)
    return out


def _epi_plain(accs, rows, cols, j):
    return (accs[0],)


def _epi_swiglu(accs, rows, cols, j):
    return (_silu(accs[0]) * accs[1],)


def _epi_ln(accs, rows, cols, j):
    z = ALPHA * rows[0] + accs[0]
    mu = jnp.mean(z, axis=-1, keepdims=True)
    d = z - mu
    var = jnp.mean(d * d, axis=-1, keepdims=True)
    y = d * lax.rsqrt(var + LN_EPS) * cols[0] + cols[1]
    return y, y


def _mm_ln(h, w, lead, xf, g, b, *, name, tm=512):
    D = xf.shape[1]
    return _mm(h, [(w, lead, 0)], _epi_ln, [(D, F32), (D, BF16)], tm=tm, tn=D, name=name,
               rows=[(xf, D, lambda j: 0)], cols=[(g.reshape(1, D), 0), (b.reshape(1, D), 0)])


def _dense_ffn(xf, xb, w_gu, w_down, layer, g, b):
    f = w_down.shape[1]
    tn = 256
    (h,) = _mm(xb, [(w_gu, (layer,), 0), (w_gu, (layer,), f // tn)], _epi_swiglu, [(f, BF16)],
               tm=1024, tn=tn, name="ffn_up")
    return _mm_ln(h, w_down, (layer,), xf, g, b, name="ffn_down_ln")


def _short_conv_mixer(xf, xb, seq, w_in, conv_w, w_out, g, b):
    D = xf.shape[1]
    tn = 256
    nb = D // tn

    def epi(accs, rows, cols, j):
        b_gate, c_gate, v = accs
        cw = cols[0]
        cv = c_gate * v
        y = cw[2:3] * cv + cw[1:2] * _shift_rows(cv, 1) + cw[0:1] * _shift_rows(cv, 2)
        return (b_gate * y,)

    (y,) = _mm(xb, [(w_in, (), 0), (w_in, (), nb), (w_in, (), 2 * nb)], epi, [(D, BF16)],
               tm=seq, tn=tn, name="conv_in", cols=[(conv_w, 0)])
    return _mm_ln(y, w_out, (), xf, g, b, name="conv_out_ln")


def _rwkv_mix(xf, seq, mix):
    T, D = xf.shape
    tn = 256
    n_mix = mix.shape[0]

    def kern(x_ref, mix_ref, *out_refs):
        x = x_ref[...]
        xx = _shift_rows(x, 1) - x
        m = mix_ref[...]
        for k, o in enumerate(out_refs):
            o[...] = (x + xx * m[k:k + 1]).astype(o.dtype)

    return pl.pallas_call(
        kern,
        grid=(T // seq, D // tn),
        in_specs=[pl.BlockSpec((seq, tn), lambda bi, j: (bi, j)),
                  pl.BlockSpec((n_mix, tn), lambda bi, j: (0, j))],
        out_specs=[pl.BlockSpec((seq, tn), lambda bi, j: (bi, j)) for _ in range(n_mix)],
        out_shape=[jax.ShapeDtypeStruct((T, D), BF16) for _ in range(n_mix)],
        compiler_params=_cparams(2),
        name="rwkv_mix",
    )(xf, mix)


def _rwkv_lora(xw, xa, xg, w0, w1, w2, a0, a1, a2, g1, g2, *, tm=512):
    T, D = xw.shape
    ws = (w1, w2, a1, a2, g1, g2)

    def kern(xw_ref, xa_ref, xg_ref, w0_ref, a0_ref, *refs):
        w_refs = refs[:6]
        ld_ref, a_ref, g_ref = refs[6:9]
        wb = refs[9:]

        @pl.when(pl.program_id(0) == 0)
        def _():
            for w_ref, b_ref in zip(w_refs, wb):
                b_ref[...] = w_ref[...].astype(BF16)

        def dot(u, w):
            return jnp.dot(u.astype(BF16), w[...], preferred_element_type=F32)

        zw = w0_ref[...] + dot(jnp.tanh(dot(xw_ref[...], wb[0])), wb[1])
        w_log = -(jnp.maximum(-zw, 0.0) + jnp.log(1.0 + jnp.exp(-jnp.abs(zw)))) - 0.5
        ld_ref[...] = -jnp.exp(w_log)
        a_ref[...] = _sigmoid(a0_ref[...] + dot(dot(xa_ref[...], wb[2]), wb[3]))
        g_ref[...] = dot(_sigmoid(dot(xg_ref[...], wb[4])), wb[5])

    row = pl.BlockSpec((tm, D), lambda i: (i, 0))
    full = lambda a: pl.BlockSpec(a.shape, lambda i: (0, 0))
    vec = pl.BlockSpec((1, D), lambda i: (0, 0))
    return pl.pallas_call(
        kern,
        grid=(T // tm,),
        in_specs=[row, row, row, vec, vec] + [full(w) for w in ws],
        out_specs=[row, row, row],
        out_shape=[jax.ShapeDtypeStruct((T, D), F32) for _ in range(3)],
        scratch_shapes=[pltpu.VMEM(w.shape, BF16) for w in ws],
        compiler_params=_cparams(1),
        name="rwkv_lora",
    )(xw, xa, xg, w0.reshape(1, D), a0.reshape(1, D), *ws)


def _rwkv_scan(r, k, v, ld, a, g, seq, k_k, k_a, r_k, gn_g, gn_b):
    T, D = r.shape
    C = RWKV_CHUNK
    H, N = RWKV_HEADS, RWKV_HEAD
    n_c = seq // C

    def bdot(x, y):
        return jnp.dot(x.astype(BF16), y.astype(BF16), preferred_element_type=F32)

    def kern(r_ref, k_ref, v_ref, ld_ref, a_ref, g_ref, kk_ref, ka_ref, rk_ref, gg_ref, gb_ref,
             o_ref, st_ref):
        @pl.when(pl.program_id(1) == 0)
        def _():
            st_ref[...] = jnp.zeros_like(st_ref)

        ri = lax.broadcasted_iota(jnp.int32, (C, C), 0)
        ci = lax.broadcasted_iota(jnp.int32, (C, C), 1)
        strict = ri > ci
        incl = ri >= ci
        eye = ri == ci
        tri = incl.astype(F32)
        ld_all = ld_ref[...]
        cum_all = jnp.dot(tri, ld_all, precision=lax.Precision.HIGHEST,
                          preferred_element_type=F32)

        for h in range(H):
            sl = slice(h * N, (h + 1) * N)
            rh, kh, vh, ah = r_ref[:, sl], k_ref[:, sl], v_ref[:, sl], a_ref[:, sl]
            ldh, cum = ld_all[:, sl], cum_all[:, sl]
            kk = kh * kk_ref[:, sl]
            nrm = jnp.sqrt(jnp.sum(kk * kk, axis=-1, keepdims=True))
            kk = kk / jnp.maximum(nrm, 1e-12)
            km = kh * (1.0 + (ah - 1.0) * ka_ref[:, sl])
            gam = jnp.exp(cum)
            gam_inv = jnp.exp(-cum)
            gam_last = gam[C - 1:C, :]
            at = -kk * jnp.exp(cum - ldh)
            bt = kk * ah * gam_inv
            kt = km * gam_inv
            rt = rh * gam
            ar = jnp.concatenate([at, rt], axis=0)
            bk = jnp.concatenate([bt, kt], axis=0)
            m = lax.dot_general(ar.astype(BF16), bk.astype(BF16), (((1,), (1,)), ((), ())),
                                preferred_element_type=F32)
            st = st_ref[h]
            ars = bdot(ar, st)
            lmat = jnp.where(strict, m[:C, :C], 0.0)
            u = ars[:C] + bdot(jnp.where(strict, m[:C, C:], 0.0), vh)
            n = 1
            while n < C:
                u = u + bdot(lmat, u)
                n *= 2
                if n < C:
                    lmat = bdot(lmat, lmat)
            uv = jnp.concatenate([u, vh], axis=0)
            rmat = jnp.concatenate([jnp.where(incl, m[C:, :C], 0.0),
                                    jnp.where(incl, m[C:, C:], 0.0)], axis=1)
            y = ars[C:] + bdot(rmat, uv)
            gam_col = jnp.sum(jnp.where(eye, gam_last, 0.0), axis=1, keepdims=True)
            st_ref[h] = gam_col * st + lax.dot_general(
                (bk * gam_last).astype(BF16), uv.astype(BF16), (((0,), (0,)), ((), ())),
                preferred_element_type=F32)

            mu = jnp.mean(y, axis=-1, keepdims=True)
            dy = y - mu
            var = jnp.mean(dy * dy, axis=-1, keepdims=True)
            yn = dy * lax.rsqrt(var + RWKV_GN_EPS) * gg_ref[:, sl] + gb_ref[:, sl]
            bonus = jnp.sum(rh * km * rk_ref[:, sl], axis=-1, keepdims=True) * vh
            o_ref[:, sl] = ((yn + bonus) * g_ref[:, sl]).astype(o_ref.dtype)

    row = pl.BlockSpec((C, D), lambda bi, c: (bi * n_c + c, 0))
    vec = pl.BlockSpec((1, D), lambda bi, c: (0, 0))
    return pl.pallas_call(
        kern,
        grid=(T // seq, n_c),
        in_specs=[row] * 6 + [vec] * 5,
        out_specs=row,
        out_shape=jax.ShapeDtypeStruct((T, D), BF16),
        scratch_shapes=[pltpu.VMEM((H, N, N), F32)],
        compiler_params=_cparams(2),
        name="rwkv_scan",
    )(r, k, v, ld, a, g, k_k.reshape(1, D), k_a.reshape(1, D), r_k.reshape(1, D),
      gn_g.reshape(1, D), gn_b.reshape(1, D))


def _rwkv7_mixer(xf, seq, mix, w_r, w_k, w_v, w0, w1, w2, a0, a1, a2, g1, g2,
                 k_k, k_a, r_k, gn_g, gn_b, w_o, g, b):
    D = xf.shape[1]
    xr, xw, xk, xv, xa, xg = _rwkv_mix(xf, seq, mix)
    proj = lambda u, w: _mm(u, [(w, (), 0)], _epi_plain, [(D, F32)], tm=1024, tn=512,
                            name="rwkv_proj")[0]
    r, k, v = proj(xr, w_r), proj(xk, w_k), proj(xv, w_v)
    ld, a, gate = _rwkv_lora(xw, xa, xg, w0, w1, w2, a0, a1, a2, g1, g2)
    y = _rwkv_scan(r, k, v, ld, a, gate, seq, k_k, k_a, r_k, gn_g, gn_b)
    return _mm_ln(y, w_o, (), xf, g, b, name="rwkv_out_ln")


def _rglru_core(proj, pos3, seq, conv_w, conv_b, w_ga, b_ga, w_gx, b_gx, lam):
    T = proj.shape[0]
    W = LRU_BW
    n_conv = conv_w.shape[0]

    def kern(gate_ref, u_ref, pos_ref, cw_ref, cb_ref, wga_ref, bga_ref, wgx_ref, bgx_ref,
             lam_ref, o_ref):
        u = u_ref[...]
        cw = cw_ref[...]
        uc = cw[n_conv - 1:n_conv] * u + cb_ref[...]
        for d in range(1, n_conv):
            uc = uc + cw[n_conv - 1 - d:n_conv - d] * _shift_rows(u, d)
        ub = uc.astype(BF16)
        rg = _sigmoid(jnp.dot(ub, wga_ref[...].astype(BF16), preferred_element_type=F32)
                      + bga_ref[...])
        ig = _sigmoid(jnp.dot(ub, wgx_ref[...].astype(BF16), preferred_element_type=F32)
                      + bgx_ref[...])
        nl = -lam_ref[...]
        softplus = jnp.maximum(nl, 0.0) + jnp.log1p(jnp.exp(-jnp.abs(nl)))
        log_a = -LRU_C * rg * softplus
        reset = pos_ref[...] == 0
        a = jnp.where(reset, 0.0, jnp.exp(log_a))
        mult = jnp.where(reset, 1.0, jnp.sqrt(-_expm1(2.0 * log_a)))
        hb = mult * (ig * uc)
        d = 1
        while d < seq:
            hb = hb + a * _shift_rows(hb, d)
            if 2 * d < seq:
                a = a * _shift_rows(a, d, 1.0)
            d *= 2
        gt = gate_ref[...]
        gelu = 0.5 * gt * (1.0 + jnp.tanh(math.sqrt(2.0 / math.pi) * (gt + 0.044715 * gt * gt * gt)))
        o_ref[...] = (gelu * hb).astype(o_ref.dtype)

    vec = pl.BlockSpec((1, W), lambda bi, j: (0, j))
    blk = pl.BlockSpec((None, W, W), lambda bi, j: (j, 0, 0))
    return pl.pallas_call(
        kern,
        grid=(T // seq, LRU_BLOCKS),
        in_specs=[pl.BlockSpec((seq, W), lambda bi, j: (bi, j)),
                  pl.BlockSpec((seq, W), lambda bi, j: (bi, LRU_BLOCKS + j)),
                  pl.BlockSpec((None, seq, 1), lambda bi, j: (bi, 0, 0)),
                  pl.BlockSpec((n_conv, W), lambda bi, j: (0, j)),
                  vec, blk, vec, blk, vec, vec],
        out_specs=pl.BlockSpec((seq, W), lambda bi, j: (bi, j)),
        out_shape=jax.ShapeDtypeStruct((T, D_RNN), BF16),
        compiler_params=_cparams(2),
        name="rglru_core",
    )(proj, proj, pos3, conv_w, conv_b.reshape(1, -1), w_ga, b_ga.reshape(1, -1),
      w_gx, b_gx.reshape(1, -1), lam.reshape(1, -1))


def _rglru_mixer(xf, xb, pos3, seq, w_in, conv_w, conv_b, w_ga, b_ga, w_gx, b_gx, lam, w_out, g, b):
    (proj,) = _mm(xb, [(w_in, (), 0)], _epi_plain, [(2 * D_RNN, F32)], tm=1024, tn=512,
                  name="rglru_in")
    y = _rglru_core(proj, pos3, seq, conv_w, conv_b, w_ga, b_ga, w_gx, b_gx, lam)
    return _mm_ln(y, w_out, (), xf, g, b, name="rglru_out_ln")


def _rope_tables(pos_col, freq, *, tm=2048):
    T = pos_col.shape[0]
    half = freq.shape[1]

    def kern(pos_ref, f_ref, cos_ref, sin_ref):
        ang = pos_ref[...].astype(F32) * f_ref[...]
        cos_ref[...] = jnp.cos(ang)
        sin_ref[...] = jnp.sin(ang)

    out = pl.BlockSpec((tm, half), lambda i: (i, 0))
    return pl.pallas_call(
        kern,
        grid=(T // tm,),
        in_specs=[pl.BlockSpec((tm, 1), lambda i: (i, 0)), pl.BlockSpec((1, half), lambda i: (0, 0))],
        out_specs=[out, out],
        out_shape=[jax.ShapeDtypeStruct((T, half), F32)] * 2,
        compiler_params=_cparams(1),
        name="rope_tables",
    )(pos_col, freq)


def _retention_core(qk, vg, seq):
    T = qk.shape[0]
    L, CH = RET_SUPER, RET_CHUNK
    H, DK, DV = RET_HEADS, RET_DK, RET_DV
    n_l = seq // L

    def kern(q_ref, k_ref, v_ref, g_ref, o_ref, st_ref):
        h = pl.program_id(1)

        @pl.when(pl.program_id(2) == 0)
        def _():
            st_ref[...] = jnp.zeros_like(st_ref)

        hv = jnp.zeros((1, 1), jnp.int32) + h
        log_g = jnp.zeros((1, 1), F32)
        for hh in range(H):
            log_g = jnp.where(hv == hh, math.log1p(-(2.0 ** (-5.0 - hh))), log_g)
        ri = lax.broadcasted_iota(jnp.int32, (L, L), 0)
        ci = lax.broadcasted_iota(jnp.int32, (L, L), 1)
        dist = jnp.abs(ri - ci).astype(F32)
        decay = jnp.where(ci // CH <= ri // CH, jnp.exp(dist * log_g), 0.0)
        idx = lax.broadcasted_iota(jnp.int32, (L, 1), 0).astype(F32)
        q_dec = jnp.exp((idx + 1.0) * log_g)
        k_dec = jnp.exp((L - 1.0 - idx) * log_g)
        c_dec = jnp.exp(L * log_g)

        q = q_ref[...]
        k = k_ref[...]
        v = v_ref[...]
        st = st_ref[...]
        scores = lax.dot_general(q, k, (((1,), (1,)), ((), ())), preferred_element_type=F32)
        scores = (scores * decay).astype(BF16)
        qd = (q.astype(F32) * q_dec).astype(BF16)
        y = (jnp.dot(scores, v, preferred_element_type=F32)
             + jnp.dot(qd, st.astype(BF16), preferred_element_type=F32))
        kd = (k.astype(F32) * k_dec).astype(BF16)
        st_ref[...] = st * c_dec + lax.dot_general(kd, v, (((0,), (0,)), ((), ())),
                                                   preferred_element_type=F32)
        yn = y * lax.rsqrt(jnp.mean(y * y, axis=-1, keepdims=True) + RET_GN_EPS)
        o_ref[...] = (_silu(g_ref[...].astype(F32)) * yn).astype(o_ref.dtype)

    return pl.pallas_call(
        kern,
        grid=(T // seq, H, n_l),
        in_specs=[pl.BlockSpec((L, DK), lambda bi, h, l: (bi * n_l + l, h)),
                  pl.BlockSpec((L, DK), lambda bi, h, l: (bi * n_l + l, H + h)),
                  pl.BlockSpec((L, DV), lambda bi, h, l: (bi * n_l + l, h)),
                  pl.BlockSpec((L, DV), lambda bi, h, l: (bi * n_l + l, H + h))],
        out_specs=pl.BlockSpec((L, DV), lambda bi, h, l: (bi * n_l + l, h)),
        out_shape=jax.ShapeDtypeStruct((T, H * DV), BF16),
        scratch_shapes=[pltpu.VMEM((DK, DV), F32)],
        compiler_params=_cparams(3),
        name="retention_core",
    )(qk, qk, vg, vg)


def _retention_mixer(xf, xb, pos_col, seq, w_in, w_o, g, b):
    D = xf.shape[1]
    half = RET_DK // 2
    freq = (ROPE_BASE ** -jnp.linspace(0.0, 1.0, half, dtype=F32)).reshape(1, half)
    cos, sin = _rope_tables(pos_col, freq)
    n_q = D // RET_DK

    def epi_rope(accs, rows, cols, j):
        t1, t2 = accs[0][:, :half], accs[0][:, half:]
        c, s = rows
        scale = jnp.where(j >= n_q, RET_DK ** -0.5, 1.0).astype(F32)
        return (jnp.concatenate([t1 * c - t2 * s, t1 * s + t2 * c], axis=1) * scale,)

    (qk,) = _mm(xb, [(w_in, (), 0)], epi_rope, [(2 * D, BF16)], tm=1024, tn=RET_DK,
                name="ret_qk_rope", rows=[(cos, half, lambda j: 0), (sin, half, lambda j: 0)])
    (vg,) = _mm(xb, [(w_in, (), 2 * D // 512)], _epi_plain, [(4 * D, BF16)], tm=1024, tn=512,
                name="ret_vg")
    y = _retention_core(qk, vg, seq)
    return _mm_ln(y, w_o, (), xf, g, b, name="ret_out_ln")


def _moe_router(xf, w_router, *, tm=1024):
    T, D = xf.shape
    E = w_router.shape[1]
    LANES = 128
    wpad = jnp.pad(w_router, ((0, 0), (0, LANES - E)))

    def kern(x_ref, w_ref, id_ref, p_ref):
        logits = jnp.dot(x_ref[...], w_ref[...], precision=lax.Precision.HIGHEST,
                         preferred_element_type=F32)
        lane = lax.broadcasted_iota(jnp.int32, logits.shape, 1).astype(F32)
        neg = jnp.float32(-jnp.inf)
        l1 = jnp.where(lane < E, logits, neg)
        m1 = jnp.max(l1, axis=-1, keepdims=True)
        i1 = jnp.min(jnp.where(l1 == m1, lane, float(LANES)), axis=-1, keepdims=True)
        l2 = jnp.where(lane == i1, neg, l1)
        m2 = jnp.max(l2, axis=-1, keepdims=True)
        i2 = jnp.min(jnp.where(l2 == m2, lane, float(LANES)), axis=-1, keepdims=True)
        e = jnp.exp(m2 - m1)
        p1 = 1.0 / (1.0 + e)
        p2 = e / (1.0 + e)
        id_ref[...] = jnp.where(lane == 0, i1, jnp.where(lane == 1, i2, 0.0)).astype(jnp.int32)
        p_ref[...] = jnp.where(lane == 0, p1, jnp.where(lane == 1, p2, 0.0))

    out = pl.BlockSpec((tm, LANES), lambda i: (i, 0))
    return pl.pallas_call(
        kern,
        grid=(T // tm,),
        in_specs=[pl.BlockSpec((tm, D), lambda i: (i, 0)), pl.BlockSpec((D, LANES), lambda i: (0, 0))],
        out_specs=[out, out],
        out_shape=[jax.ShapeDtypeStruct((T, LANES), jnp.int32), jax.ShapeDtypeStruct((T, LANES), F32)],
        compiler_params=_cparams(1),
        name="moe_router",
    )(xf, wpad)


def _moe_plan(ids, n_tiles):
    T = ids.shape[0]
    e_flat = ids.reshape(-1)
    onehot = (e_flat[:, None] == jnp.arange(N_EXPERTS, dtype=jnp.int32)[None, :]).astype(jnp.int32)
    csum = jnp.cumsum(onehot, axis=0)
    rank = jnp.sum((csum - onehot) * onehot, axis=1)
    counts = csum[-1]
    padded = ((counts + MOE_TILE - 1) // MOE_TILE) * MOE_TILE
    ends = jnp.cumsum(padded)
    starts = ends - padded
    pos = (jnp.sum(onehot * starts[None, :], axis=1) + rank).astype(jnp.int32)
    tile_start = jnp.arange(n_tiles, dtype=jnp.int32) * MOE_TILE
    tile_e = jnp.sum((tile_start[:, None] >= ends[None, :]).astype(jnp.int32), axis=1)
    tile_e = jnp.minimum(tile_e, N_EXPERTS - 1).astype(jnp.int32)
    n_used = (ends[-1] // MOE_TILE).astype(jnp.int32).reshape(1)
    return pos.reshape(T, TOP_K), tile_e, n_used


def _moe_dispatch(xf, pos, n_rows, *, tq=256):
    T, D = xf.shape
    n_steps = T // tq
    pos_blocks = pos.reshape(n_steps, 1, tq * TOP_K)

    def kern(pos_ref, x_ref, init_ref, xs_ref, sem):
        del init_ref

        def copy(r, k):
            dst = pos_ref[0, TOP_K * r + k]
            return pltpu.make_async_copy(x_ref.at[pl.ds(r, 1)], xs_ref.at[pl.ds(dst, 1)], sem)

        def start(r, c):
            for k in range(TOP_K):
                copy(r, k).start()
            return c

        def wait(r, c):
            for k in range(TOP_K):
                copy(r, k).wait()
            return c

        lax.fori_loop(0, tq, start, 0)
        lax.fori_loop(0, tq, wait, 0)

    init = jnp.zeros((n_rows, D), F32)
    return pl.pallas_call(
        kern,
        grid=(n_steps,),
        in_specs=[pl.BlockSpec((None, 1, tq * TOP_K), lambda i: (i, 0, 0), memory_space=pltpu.SMEM),
                  pl.BlockSpec((tq, D), lambda i: (i, 0)),
                  pl.BlockSpec(memory_space=pl.ANY)],
        out_specs=pl.BlockSpec(memory_space=pl.ANY),
        out_shape=jax.ShapeDtypeStruct((n_rows, D), F32),
        scratch_shapes=[pltpu.SemaphoreType.DMA],
        input_output_aliases={2: 0},
        compiler_params=_cparams(1),
        name="moe_dispatch",
    )(pos_blocks, xf, init)


def _moe_experts(xs, tile_e, n_used, w_gu, w_down, layer):
    P, D = xs.shape
    F = w_down.shape[2]
    tm = MOE_TILE
    n_tiles = P // tm
    tn_up = 896
    n_f = F // tn_up
    tn_dn = 512

    def tile(i, nu):
        return jnp.minimum(i, nu[0] - 1)

    def fresh(te, nu):
        i = pl.program_id(1)
        prev = te[jnp.maximum(i - 1, 0)]
        return jnp.logical_and(i < nu[0], jnp.logical_or(i == 0, te[i] != prev))

    def up_kern(te, nu, x_ref, wg_ref, wu_ref, h_ref, wgb, wub):
        @pl.when(fresh(te, nu))
        def _():
            wgb[...] = wg_ref[...].astype(BF16)
            wub[...] = wu_ref[...].astype(BF16)

        @pl.when(pl.program_id(1) < nu[0])
        def _():
            xv = x_ref[...].astype(BF16)
            gt = jnp.dot(xv, wgb[...], preferred_element_type=F32)
            up = jnp.dot(xv, wub[...], preferred_element_type=F32)
            h_ref[...] = (_silu(gt) * up).astype(h_ref.dtype)

        @pl.when(pl.program_id(1) >= nu[0])
        def _():
            h_ref[...] = jnp.zeros_like(h_ref)

    h = pl.pallas_call(
        up_kern,
        grid_spec=pltpu.PrefetchScalarGridSpec(
            num_scalar_prefetch=2,
            grid=(n_f, n_tiles),
            in_specs=[pl.BlockSpec((tm, D), lambda f, i, te, nu: (tile(i, nu), 0)),
                      pl.BlockSpec((None, None, D, tn_up),
                                   lambda f, i, te, nu: (layer, te[tile(i, nu)], 0, f)),
                      pl.BlockSpec((None, None, D, tn_up),
                                   lambda f, i, te, nu: (layer, te[tile(i, nu)], 0, n_f + f))],
            out_specs=pl.BlockSpec((tm, tn_up), lambda f, i, te, nu: (i, f)),
            scratch_shapes=[pltpu.VMEM((D, tn_up), BF16), pltpu.VMEM((D, tn_up), BF16)]),
        out_shape=jax.ShapeDtypeStruct((P, F), BF16),
        compiler_params=_cparams(2),
        name="moe_up",
    )(tile_e, n_used, xs, w_gu, w_gu)

    def down_kern(te, nu, h_ref, wd_ref, y_ref, wdb):
        @pl.when(fresh(te, nu))
        def _():
            wdb[...] = wd_ref[...].astype(BF16)

        @pl.when(pl.program_id(1) < nu[0])
        def _():
            y_ref[...] = jnp.dot(h_ref[...], wdb[...], preferred_element_type=F32)

        @pl.when(pl.program_id(1) >= nu[0])
        def _():
            y_ref[...] = jnp.zeros_like(y_ref)

    return pl.pallas_call(
        down_kern,
        grid_spec=pltpu.PrefetchScalarGridSpec(
            num_scalar_prefetch=2,
            grid=(D // tn_dn, n_tiles),
            in_specs=[pl.BlockSpec((tm, F), lambda n, i, te, nu: (tile(i, nu), 0)),
                      pl.BlockSpec((None, None, F, tn_dn),
                                   lambda n, i, te, nu: (layer, te[tile(i, nu)], 0, n))],
            out_specs=pl.BlockSpec((tm, tn_dn), lambda n, i, te, nu: (i, n)),
            scratch_shapes=[pltpu.VMEM((F, tn_dn), BF16)]),
        out_shape=jax.ShapeDtypeStruct((P, D), F32),
        compiler_params=_cparams(2),
        name="moe_down",
    )(tile_e, n_used, h, w_down)


def _moe_combine_ln(xf, ys, pos, probs, g, b, *, tq=256):
    T, D = xf.shape
    n_steps = T // tq
    pos_blocks = pos.reshape(n_steps, 1, tq * TOP_K)

    def kern(pos_ref, x_ref, p_ref, g_ref, b_ref, ys_ref, of_ref, ob_ref, buf, sem):
        def copy(r, k):
            src = pos_ref[0, TOP_K * r + k]
            return pltpu.make_async_copy(ys_ref.at[pl.ds(src, 1)], buf.at[k, pl.ds(r, 1)], sem)

        def start(r, c):
            for k in range(TOP_K):
                copy(r, k).start()
            return c

        def wait(r, c):
            for k in range(TOP_K):
                copy(r, k).wait()
            return c

        lax.fori_loop(0, tq, start, 0)
        lax.fori_loop(0, tq, wait, 0)
        p = p_ref[...]
        z = ALPHA * x_ref[...] + p[:, 0:1] * buf[0] + p[:, 1:2] * buf[1]
        mu = jnp.mean(z, axis=-1, keepdims=True)
        d = z - mu
        var = jnp.mean(d * d, axis=-1, keepdims=True)
        y = d * lax.rsqrt(var + LN_EPS) * g_ref[...] + b_ref[...]
        of_ref[...] = y
        ob_ref[...] = y.astype(ob_ref.dtype)

    row = pl.BlockSpec((tq, D), lambda i: (i, 0))
    vec = pl.BlockSpec((1, D), lambda i: (0, 0))
    return pl.pallas_call(
        kern,
        grid=(n_steps,),
        in_specs=[pl.BlockSpec((None, 1, tq * TOP_K), lambda i: (i, 0, 0), memory_space=pltpu.SMEM),
                  row, pl.BlockSpec((tq, probs.shape[1]), lambda i: (i, 0)), vec, vec,
                  pl.BlockSpec(memory_space=pl.ANY)],
        out_specs=[row, row],
        out_shape=[jax.ShapeDtypeStruct((T, D), F32), jax.ShapeDtypeStruct((T, D), BF16)],
        scratch_shapes=[pltpu.VMEM((TOP_K, tq, D), F32), pltpu.SemaphoreType.DMA],
        compiler_params=_cparams(1),
        name="moe_combine_ln",
    )(pos_blocks, xf, probs, g.reshape(1, D), b.reshape(1, D), ys)


def _moe_ffn(xf, w_router, w_gu, w_down, layer, g, b):
    T = xf.shape[0]
    n_tiles = T * TOP_K // MOE_TILE + N_EXPERTS
    ids, probs = _moe_router(xf, w_router[layer])
    pos, tile_e, n_used = _moe_plan(ids[:, :TOP_K], n_tiles)
    xs = _moe_dispatch(xf, pos, n_tiles * MOE_TILE)
    ys = _moe_experts(xs, tile_e, n_used, w_gu, w_down, layer)
    return _moe_combine_ln(xf, ys, pos, probs, g, b)


def kernel(x, positions, ln_mix_g, ln_mix_b, ln_ffn_g, ln_ffn_b, a_w_in, a_conv_w, a_w_out, b_mix, b_w_r, b_w_k, b_w_v, b_w0, b_w1, b_w2, b_a0, b_a1, b_a2, b_g1, b_g2, b_k_k, b_k_a, b_r_k, b_gn_g, b_gn_b, b_w_o, c_w_in, c_conv_w, c_conv_b, c_w_ga, c_b_ga, c_w_gx, c_b_gx, c_lam, c_w_out, d_w_in, d_w_o, ffn_w_gu, ffn_w_down, moe_w_router, moe_w_gu, moe_w_down):
    bsz, seq, D = x.shape
    T = bsz * seq
    xf = x.reshape(T, D)
    xb = xf.astype(BF16)
    pos3 = positions.reshape(bsz, seq, 1)
    pos_col = positions.reshape(T, 1)

    xf, xb = _short_conv_mixer(xf, xb, seq, a_w_in, a_conv_w, a_w_out, ln_mix_g[0], ln_mix_b[0])
    xf, xb = _dense_ffn(xf, xb, ffn_w_gu, ffn_w_down, 0, ln_ffn_g[0], ln_ffn_b[0])

    xf, xb = _rwkv7_mixer(xf, seq, b_mix, b_w_r, b_w_k, b_w_v, b_w0, b_w1, b_w2, b_a0, b_a1, b_a2,
                          b_g1, b_g2, b_k_k, b_k_a, b_r_k, b_gn_g, b_gn_b, b_w_o,
                          ln_mix_g[1], ln_mix_b[1])
    xf, xb = _moe_ffn(xf, moe_w_router, moe_w_gu, moe_w_down, 0, ln_ffn_g[1], ln_ffn_b[1])

    xf, xb = _rglru_mixer(xf, xb, pos3, seq, c_w_in, c_conv_w, c_conv_b, c_w_ga, c_b_ga, c_w_gx,
                          c_b_gx, c_lam, c_w_out, ln_mix_g[2], ln_mix_b[2])
    xf, xb = _dense_ffn(xf, xb, ffn_w_gu, ffn_w_down, 1, ln_ffn_g[2], ln_ffn_b[2])

    xf, xb = _retention_mixer(xf, xb, pos_col, seq, d_w_in, d_w_o, ln_mix_g[3], ln_mix_b[3])
    xf, xb = _moe_ffn(xf, moe_w_router, moe_w_gu, moe_w_down, 1, ln_ffn_g[3], ln_ffn_b[3])
    return xf.reshape(bsz, seq, D)
```

```python
import functools
import math

import jax
import jax.numpy as jnp
from jax import lax
from jax.experimental import pallas as pl
from jax.experimental.pallas import tpu as pltpu

F32 = jnp.float32
BF16 = jnp.bfloat16

D_MODEL = 1024
DEPTH = 4
ALPHA = (2.0 * DEPTH) ** 0.25
LN_EPS = 1e-5

RWKV_HEAD = 64
RWKV_HEADS = D_MODEL // RWKV_HEAD
RWKV_GN_EPS = 64e-5
RWKV_CHUNK = 64

D_RNN = 1280
LRU_BLOCKS = 10
LRU_BW = D_RNN // LRU_BLOCKS
LRU_C = 8.0

RET_HEADS = 4
RET_DK = D_MODEL // RET_HEADS
RET_DV = 2 * D_MODEL // RET_HEADS
RET_CHUNK = 64
RET_SUPER = 256
ROPE_BASE = 10000.0
RET_GN_EPS = 1e-6

N_EXPERTS = 8
TOP_K = 2
MOE_TILE = 512
DMA_UNROLL = 8

VMEM_LIMIT = 56 * 1024 * 1024
F32_SUBLANES = 8


def _cparams(n_grid):
    return pltpu.CompilerParams(dimension_semantics=("arbitrary",) * n_grid,
                                vmem_limit_bytes=VMEM_LIMIT)


def _sigmoid(x):
    return 1.0 / (1.0 + jnp.exp(-x))


def _silu(x):
    return x * _sigmoid(x)


def _expm1(z):
    u = jnp.exp(z)
    safe_u = jnp.where(u == 1.0, 0.5, jnp.maximum(u, 0.5))
    small = (u - 1.0) * z / jnp.log(safe_u)
    return jnp.where(u == 1.0, z, jnp.where(u < 0.5, u - 1.0, small))


def _shift_rows(x, d, fill=0.0):
    rows = lax.broadcasted_iota(jnp.int32, x.shape, 0)
    return jnp.where(rows >= d, pltpu.roll(x, d, axis=0), fill)


def _scan_steps(a, b, axis, need_a):
    n = a.shape[axis]
    idx = lax.broadcasted_iota(jnp.int32, a.shape, axis)
    d = 1
    while d < n:
        keep = idx >= d
        b = b + a * jnp.where(keep, pltpu.roll(b, d, axis=axis), 0.0)
        if need_a or 2 * d < n:
            a = a * jnp.where(keep, pltpu.roll(a, d, axis=axis), 1.0)
        d *= 2
    return a, b


def _linear_scan_rows(a, b, a_ref, b_ref, c_ref):
    s = a.shape[0]
    n_sub = F32_SUBLANES
    g = s // n_sub
    sub = lax.broadcasted_iota(jnp.int32, a.shape, 0) % n_sub
    d = 1
    while d < n_sub:
        keep = sub >= d
        b = b + a * jnp.where(keep, pltpu.roll(b, d, axis=0), 0.0)
        a = a * jnp.where(keep, pltpu.roll(a, d, axis=0), 1.0)
        d *= 2
    a_ref[...] = a
    b_ref[...] = b
    last = pl.ds(n_sub - 1, g, stride=n_sub)
    _, tot = _scan_steps(a_ref[last, :], b_ref[last, :], 0, False)
    carry = _shift_rows(tot, 1)
    for j in range(n_sub):
        c_ref[pl.ds(j, g, stride=n_sub), :] = carry
    return b + a * c_ref[...]


def _mm(x, ws, epi, outs, *, tm, tn, name, rows=(), cols=()):
    M, K = x.shape
    n_j = outs[0][0] // tn
    n_i = M // tm
    assert M % tm == 0 and all(o[0] == n_j * tn for o in outs)
    nw = len(ws)

    in_specs = [pl.BlockSpec((tm, K), lambda j, i: (i, 0))]
    args = [x]
    for w, lead, off in ws:
        nl = len(lead)
        in_specs.append(pl.BlockSpec((None,) * nl + (K, tn),
                                     lambda j, i, lead=lead, off=off: tuple(lead) + (0, off + j)))
        args.append(w)
    for a, width, col_fn in rows:
        in_specs.append(pl.BlockSpec((tm, width), lambda j, i, col_fn=col_fn: (i, col_fn(j))))
        args.append(a)
    for a, off in cols:
        in_specs.append(pl.BlockSpec((a.shape[0], tn), lambda j, i, off=off: (0, off + j)))
        args.append(a)

    def kern(*refs):
        x_ref = refs[0]
        w_refs = refs[1:1 + nw]
        row_refs = refs[1 + nw:1 + nw + len(rows)]
        col_refs = refs[1 + nw + len(rows):1 + nw + len(rows) + len(cols)]
        out_refs = refs[1 + nw + len(rows) + len(cols):-nw]
        wb_refs = refs[-nw:]

        @pl.when(pl.program_id(1) == 0)
        def _():
            for w_ref, wb in zip(w_refs, wb_refs):
                wb[...] = w_ref[...].astype(BF16)

        xv = x_ref[...]
        accs = [jnp.dot(xv, wb[...], preferred_element_type=F32) for wb in wb_refs]
        res = epi(accs, [r[...] for r in row_refs], [c[...] for c in col_refs], pl.program_id(0))
        for o, r in zip(out_refs, res):
            o[...] = r.astype(o.dtype)

    out = pl.pallas_call(
        kern,
        grid=(n_j, n_i),
        in_specs=in_specs,
        out_specs=[pl.BlockSpec((tm, tn), lambda j, i: (i, j)) for _ in outs],
        out_shape=[jax.ShapeDtypeStruct((M, n), dt) for n, dt in outs],
        scratch_shapes=[pltpu.VMEM((K, tn), BF16) for _ in ws],
        compiler_params=_cparams(2),
        name=name,
    )(*args)
    return out


def _epi_plain(accs, rows, cols, j):
    return (accs[0],)


def _epi_swiglu(accs, rows, cols, j):
    return (_silu(accs[0]) * accs[1],)


def _epi_ln(accs, rows, cols, j):
    z = ALPHA * rows[0] + accs[0]
    mu = jnp.mean(z, axis=-1, keepdims=True)
    d = z - mu
    var = jnp.mean(d * d, axis=-1, keepdims=True)
    y = d * lax.rsqrt(var + LN_EPS) * cols[0] + cols[1]
    return y, y


def _mm_ln(h, w, lead, xf, g, b, *, name, tm=512):
    D = xf.shape[1]
    return _mm(h, [(w, lead, 0)], _epi_ln, [(D, F32), (D, BF16)], tm=tm, tn=D, name=name,
               rows=[(xf, D, lambda j: 0)], cols=[(g.reshape(1, D), 0), (b.reshape(1, D), 0)])


def _dense_ffn(xf, xb, w_gu, w_down, layer, g, b):
    f = w_down.shape[1]
    tn = 256
    (h,) = _mm(xb, [(w_gu, (layer,), 0), (w_gu, (layer,), f // tn)], _epi_swiglu, [(f, BF16)],
               tm=2048, tn=tn, name="ffn_up")
    return _mm_ln(h, w_down, (layer,), xf, g, b, name="ffn_down_ln")


def _short_conv_mixer(xf, xb, seq, w_in, conv_w, w_out, g, b):
    D = xf.shape[1]
    tn = 256
    nb = D // tn

    def epi(accs, rows, cols, j):
        b_gate, c_gate, v = accs
        cw = cols[0]
        cv = c_gate * v
        y = cw[2:3] * cv + cw[1:2] * _shift_rows(cv, 1) + cw[0:1] * _shift_rows(cv, 2)
        return (b_gate * y,)

    (y,) = _mm(xb, [(w_in, (), 0), (w_in, (), nb), (w_in, (), 2 * nb)], epi, [(D, BF16)],
               tm=seq, tn=tn, name="conv_in", cols=[(conv_w, 0)])
    return _mm_ln(y, w_out, (), xf, g, b, name="conv_out_ln")


def _rwkv_mix(xf, seq, mix):
    T, D = xf.shape
    tn = 256
    n_mix = mix.shape[0]

    def kern(x_ref, mix_ref, *out_refs):
        x = x_ref[...]
        xx = _shift_rows(x, 1) - x
        m = mix_ref[...]
        for k, o in enumerate(out_refs):
            o[...] = (x + xx * m[k:k + 1]).astype(o.dtype)

    return pl.pallas_call(
        kern,
        grid=(T // seq, D // tn),
        in_specs=[pl.BlockSpec((seq, tn), lambda bi, j: (bi, j)),
                  pl.BlockSpec((n_mix, tn), lambda bi, j: (0, j))],
        out_specs=[pl.BlockSpec((seq, tn), lambda bi, j: (bi, j)) for _ in range(n_mix)],
        out_shape=[jax.ShapeDtypeStruct((T, D), BF16) for _ in range(n_mix)],
        compiler_params=_cparams(2),
        name="rwkv_mix",
    )(xf, mix)


def _rwkv_lora(xw, xa, xg, w0, w1, w2, a0, a1, a2, g1, g2, *, tm=512):
    T, D = xw.shape
    ws = (w1, w2, a1, a2, g1, g2)

    def kern(xw_ref, xa_ref, xg_ref, w0_ref, a0_ref, *refs):
        w_refs = refs[:6]
        ld_ref, a_ref, g_ref = refs[6:9]
        wb = refs[9:]

        @pl.when(pl.program_id(0) == 0)
        def _():
            for w_ref, b_ref in zip(w_refs, wb):
                b_ref[...] = w_ref[...].astype(BF16)

        def dot(u, w):
            return jnp.dot(u.astype(BF16), w[...], preferred_element_type=F32)

        zw = w0_ref[...] + dot(jnp.tanh(dot(xw_ref[...], wb[0])), wb[1])
        w_log = -(jnp.maximum(-zw, 0.0) + jnp.log(1.0 + jnp.exp(-jnp.abs(zw)))) - 0.5
        ld_ref[...] = -jnp.exp(w_log)
        a_ref[...] = _sigmoid(a0_ref[...] + dot(dot(xa_ref[...], wb[2]), wb[3]))
        g_ref[...] = dot(_sigmoid(dot(xg_ref[...], wb[4])), wb[5])

    row = pl.BlockSpec((tm, D), lambda i: (i, 0))
    full = lambda a: pl.BlockSpec(a.shape, lambda i: (0, 0))
    vec = pl.BlockSpec((1, D), lambda i: (0, 0))
    return pl.pallas_call(
        kern,
        grid=(T // tm,),
        in_specs=[row, row, row, vec, vec] + [full(w) for w in ws],
        out_specs=[row, row, row],
        out_shape=[jax.ShapeDtypeStruct((T, D), F32) for _ in range(3)],
        scratch_shapes=[pltpu.VMEM(w.shape, BF16) for w in ws],
        compiler_params=_cparams(1),
        name="rwkv_lora",
    )(xw, xa, xg, w0.reshape(1, D), a0.reshape(1, D), *ws)


def _rwkv_scan(r, k, v, ld, a, g, seq, k_k, k_a, r_k, gn_g, gn_b):
    T, D = r.shape
    C = RWKV_CHUNK
    N = RWKV_HEAD
    PW = 2 * N
    NP = D // PW
    n_c = seq // C

    def kern(r_ref, k_ref, v_ref, ld_ref, a_ref, g_ref, kk_ref, ka_ref, rk_ref, gg_ref, gb_ref,
             o_ref, st_ref):
        @pl.when(pl.program_id(1) == 0)
        def _():
            st_ref[...] = jnp.zeros_like(st_ref)

        lane = lax.broadcasted_iota(jnp.int32, (C, PW), 1)
        rowi = lax.broadcasted_iota(jnp.int32, (C, PW), 0)
        lo = lane < N
        col = jnp.where(lo, lane, lane - N)
        strict = rowi > col
        incl = rowi >= col
        lo2 = lax.broadcasted_iota(jnp.int32, (2 * C, PW), 1) < N
        kr = lax.broadcasted_iota(jnp.int32, (PW, PW), 0) < N
        kc = lax.broadcasted_iota(jnp.int32, (PW, PW), 1) < N
        same_head = kr == kc
        ones_bd = same_head.astype(BF16)
        tri = (lax.broadcasted_iota(jnp.int32, (C, C), 0)
               >= lax.broadcasted_iota(jnp.int32, (C, C), 1)).astype(F32)

        def pairs(x):
            return [x[:, p * PW:(p + 1) * PW] for p in range(NP)]

        def head_sum(x):
            xs = jnp.concatenate(pairs(x), axis=0)
            hi = xs.astype(BF16)
            rest = (xs - hi.astype(F32)).astype(BF16)
            s = (jnp.dot(hi, ones_bd, preferred_element_type=F32)
                 + jnp.dot(rest, ones_bd, preferred_element_type=F32))
            return jnp.concatenate([s[p * C:(p + 1) * C] for p in range(NP)], axis=1)

        r_all, k_all, v_all, a_all = r_ref[...], k_ref[...], v_ref[...], a_ref[...]
        ld_all = ld_ref[...]
        cum = jnp.dot(tri, ld_all, precision=lax.Precision.HIGHEST,
                      preferred_element_type=F32)
        kk = k_all * kk_ref[...]
        kk = kk / jnp.maximum(jnp.sqrt(head_sum(kk * kk)), 1e-12)
        km = k_all * (1.0 + (a_all - 1.0) * ka_ref[...])
        gam = jnp.exp(cum)
        gam_inv = jnp.exp(-cum)
        at = pairs(-kk * jnp.exp(cum - ld_all))
        bt = pairs(kk * a_all * gam_inv)
        kt = pairs(km * gam_inv)
        rt = pairs(r_all * gam)
        vp = pairs(v_all)
        gl = pairs(gam[C - 1:C, :])
        bonus = head_sum(r_all * km * rk_ref[...]) * v_all

        def nt(x, y):
            return lax.dot_general(x, y, (((1,), (1,)), ((), ())), preferred_element_type=F32)

        def split(x):
            z = jnp.zeros_like(x)
            return jnp.concatenate([jnp.where(lo, x, z), jnp.where(lo, z, x)], axis=0).astype(BF16)

        ar = [jnp.concatenate([at[p], rt[p]], axis=0) for p in range(NP)]
        ar_b = [x.astype(BF16) for x in ar]
        bk = [jnp.concatenate([bt[p], kt[p]], axis=0) for p in range(NP)]
        kb_b = [jnp.concatenate([kt[p], bt[p]], axis=0).astype(BF16) for p in range(NP)]
        st = [st_ref[p] for p in range(NP)]
        m0 = [nt(jnp.where(lo2, ar[p], 0.0).astype(BF16), bk[p].astype(BF16)) for p in range(NP)]
        m1 = [nt(jnp.where(lo2, 0.0, ar[p]).astype(BF16), kb_b[p]) for p in range(NP)]
        ars = [nt(ar_b[p], st[p].astype(BF16)) for p in range(NP)]
        lmat = [jnp.where(strict, jnp.where(lo, m0[p][:C], m1[p][:C]), 0.0).astype(BF16)
                for p in range(NP)]
        akm = [jnp.where(strict, jnp.where(lo, m1[p][:C], m0[p][:C]), 0.0).astype(BF16)
               for p in range(NP)]
        v_hi_lo = [jnp.concatenate([jnp.where(lo, 0.0, vp[p]), jnp.where(lo, vp[p], 0.0)],
                                   axis=0).astype(BF16) for p in range(NP)]
        u = [ars[p][:C] + jnp.dot(akm[p], v_hi_lo[p], preferred_element_type=F32)
             for p in range(NP)]
        n = 1
        while n < C:
            u = [u[p] + jnp.dot(lmat[p], split(u[p]), preferred_element_type=F32)
                 for p in range(NP)]
            n *= 2
            if n < C:
                lmat = [jnp.dot(lmat[p], split(lmat[p]),
                                preferred_element_type=F32).astype(BF16) for p in range(NP)]
        ys = []
        for p in range(NP):
            rmat = jnp.concatenate([jnp.where(incl, m0[p][C:], 0.0),
                                    jnp.where(incl, m1[p][C:], 0.0)], axis=1).astype(BF16)
            u0, u1 = jnp.where(lo, u[p], 0.0), jnp.where(lo, 0.0, u[p])
            v0, v1 = jnp.where(lo, vp[p], 0.0), jnp.where(lo, 0.0, vp[p])
            uvs = jnp.concatenate([u0, v0, v1, u1], axis=0).astype(BF16)
            ys.append(ars[p][C:] + jnp.dot(rmat, uvs, preferred_element_type=F32))
            uv = jnp.concatenate([u[p], vp[p]], axis=0).astype(BF16)
            upd = lax.dot_general(uv, (bk[p] * gl[p]).astype(BF16), (((0,), (0,)), ((), ())),
                                  preferred_element_type=F32)
            st_ref[p] = st[p] * gl[p] + jnp.where(same_head, upd, 0.0)

        y = jnp.concatenate(ys, axis=1)
        dy = y - head_sum(y) * (1.0 / N)
        var = head_sum(dy * dy) * (1.0 / N)
        yn = dy * lax.rsqrt(var + RWKV_GN_EPS) * gg_ref[...] + gb_ref[...]
        o_ref[...] = ((yn + bonus) * g_ref[...]).astype(o_ref.dtype)

    row = pl.BlockSpec((C, D), lambda bi, c: (bi * n_c + c, 0))
    vec = pl.BlockSpec((1, D), lambda bi, c: (0, 0))
    return pl.pallas_call(
        kern,
        grid=(T // seq, n_c),
        in_specs=[row] * 6 + [vec] * 5,
        out_specs=row,
        out_shape=jax.ShapeDtypeStruct((T, D), BF16),
        scratch_shapes=[pltpu.VMEM((NP, PW, PW), F32)],
        compiler_params=_cparams(2),
        name="rwkv_scan",
    )(r, k, v, ld, a, g, k_k.reshape(1, D), k_a.reshape(1, D), r_k.reshape(1, D),
      gn_g.reshape(1, D), gn_b.reshape(1, D))


def _rwkv7_mixer(xf, seq, mix, w_r, w_k, w_v, w0, w1, w2, a0, a1, a2, g1, g2,
                 k_k, k_a, r_k, gn_g, gn_b, w_o, g, b):
    D = xf.shape[1]
    xr, xw, xk, xv, xa, xg = _rwkv_mix(xf, seq, mix)
    proj = lambda u, w: _mm(u, [(w, (), 0)], _epi_plain, [(D, F32)], tm=1024, tn=512,
                            name="rwkv_proj")[0]
    r, k, v = proj(xr, w_r), proj(xk, w_k), proj(xv, w_v)
    ld, a, gate = _rwkv_lora(xw, xa, xg, w0, w1, w2, a0, a1, a2, g1, g2)
    y = _rwkv_scan(r, k, v, ld, a, gate, seq, k_k, k_a, r_k, gn_g, gn_b)
    return _mm_ln(y, w_o, (), xf, g, b, name="rwkv_out_ln")


def _rglru_core(proj, pos3, seq, conv_w, conv_b, w_ga, b_ga, w_gx, b_gx, lam):
    T = proj.shape[0]
    W = LRU_BW
    n_conv = conv_w.shape[0]

    def kern(gate_ref, u_ref, pos_ref, cw_ref, cb_ref, wga_ref, bga_ref, wgx_ref, bgx_ref,
             lam_ref, o_ref, sa_ref, sb_ref, sc_ref):
        u = u_ref[...]
        cw = cw_ref[...]
        uc = cw[n_conv - 1:n_conv] * u + cb_ref[...]
        for d in range(1, n_conv):
            uc = uc + cw[n_conv - 1 - d:n_conv - d] * _shift_rows(u, d)
        ub = uc.astype(BF16)
        rg = _sigmoid(jnp.dot(ub, wga_ref[...].astype(BF16), preferred_element_type=F32)
                      + bga_ref[...])
        ig = _sigmoid(jnp.dot(ub, wgx_ref[...].astype(BF16), preferred_element_type=F32)
                      + bgx_ref[...])
        nl = -lam_ref[...]
        softplus = jnp.maximum(nl, 0.0) + jnp.log1p(jnp.exp(-jnp.abs(nl)))
        log_a = -LRU_C * rg * softplus
        reset = pos_ref[...] == 0
        a = jnp.where(reset, 0.0, jnp.exp(log_a))
        mult = jnp.where(reset, 1.0, jnp.sqrt(-_expm1(2.0 * log_a)))
        hb = _linear_scan_rows(a, mult * (ig * uc), sa_ref, sb_ref, sc_ref)
        gt = gate_ref[...]
        gelu = 0.5 * gt * (1.0 + jnp.tanh(math.sqrt(2.0 / math.pi) * (gt + 0.044715 * gt * gt * gt)))
        o_ref[...] = (gelu * hb).astype(o_ref.dtype)

    vec = pl.BlockSpec((1, W), lambda bi, j: (0, j))
    blk = pl.BlockSpec((None, W, W), lambda bi, j: (j, 0, 0))
    return pl.pallas_call(
        kern,
        grid=(T // seq, LRU_BLOCKS),
        in_specs=[pl.BlockSpec((seq, W), lambda bi, j: (bi, j)),
                  pl.BlockSpec((seq, W), lambda bi, j: (bi, LRU_BLOCKS + j)),
                  pl.BlockSpec((None, seq, 1), lambda bi, j: (bi, 0, 0)),
                  pl.BlockSpec((n_conv, W), lambda bi, j: (0, j)),
                  vec, blk, vec, blk, vec, vec],
        out_specs=pl.BlockSpec((seq, W), lambda bi, j: (bi, j)),
        out_shape=jax.ShapeDtypeStruct((T, D_RNN), BF16),
        scratch_shapes=[pltpu.VMEM((seq, W), F32)] * 3,
        compiler_params=_cparams(2),
        name="rglru_core",
    )(proj, proj, pos3, conv_w, conv_b.reshape(1, -1), w_ga, b_ga.reshape(1, -1),
      w_gx, b_gx.reshape(1, -1), lam.reshape(1, -1))


def _rglru_mixer(xf, xb, pos3, seq, w_in, conv_w, conv_b, w_ga, b_ga, w_gx, b_gx, lam, w_out, g, b):
    (proj,) = _mm(xb, [(w_in, (), 0)], _epi_plain, [(2 * D_RNN, F32)], tm=1024, tn=512,
                  name="rglru_in")
    y = _rglru_core(proj, pos3, seq, conv_w, conv_b, w_ga, b_ga, w_gx, b_gx, lam)
    return _mm_ln(y, w_out, (), xf, g, b, name="rglru_out_ln")


def _rope_tables(pos_col, freq, *, tm=2048):
    T = pos_col.shape[0]
    half = freq.shape[1]

    def kern(pos_ref, f_ref, cos_ref, sin_ref):
        ang = pos_ref[...].astype(F32) * f_ref[...]
        cos_ref[...] = jnp.cos(ang)
        sin_ref[...] = jnp.sin(ang)

    out = pl.BlockSpec((tm, half), lambda i: (i, 0))
    return pl.pallas_call(
        kern,
        grid=(T // tm,),
        in_specs=[pl.BlockSpec((tm, 1), lambda i: (i, 0)), pl.BlockSpec((1, half), lambda i: (0, 0))],
        out_specs=[out, out],
        out_shape=[jax.ShapeDtypeStruct((T, half), F32)] * 2,
        compiler_params=_cparams(1),
        name="rope_tables",
    )(pos_col, freq)


def _retention_core(qk, vg, seq):
    T = qk.shape[0]
    L, CH = RET_SUPER, RET_CHUNK
    H, DK, DV = RET_HEADS, RET_DK, RET_DV
    n_l = seq // L

    def kern(q_ref, k_ref, v_ref, g_ref, o_ref, st_ref):
        h = pl.program_id(1)

        @pl.when(pl.program_id(2) == 0)
        def _():
            st_ref[...] = jnp.zeros_like(st_ref)

        hv = jnp.zeros((1, 1), jnp.int32) + h
        log_g = jnp.zeros((1, 1), F32)
        for hh in range(H):
            log_g = jnp.where(hv == hh, math.log1p(-(2.0 ** (-5.0 - hh))), log_g)
        ri = lax.broadcasted_iota(jnp.int32, (L, L), 0)
        ci = lax.broadcasted_iota(jnp.int32, (L, L), 1)
        dist = jnp.abs(ri - ci).astype(F32)
        decay = jnp.where(ci // CH <= ri // CH, jnp.exp(dist * log_g), 0.0)
        idx = lax.broadcasted_iota(jnp.int32, (L, 1), 0).astype(F32)
        q_dec = jnp.exp((idx + 1.0) * log_g)
        k_dec = jnp.exp((L - 1.0 - idx) * log_g)
        c_dec = jnp.exp(L * log_g)

        q = q_ref[...]
        k = k_ref[...]
        v = v_ref[...]
        st = st_ref[...]
        scores = lax.dot_general(q, k, (((1,), (1,)), ((), ())), preferred_element_type=F32)
        scores = (scores * decay).astype(BF16)
        qd = (q.astype(F32) * q_dec).astype(BF16)
        y = (jnp.dot(scores, v, preferred_element_type=F32)
             + jnp.dot(qd, st.astype(BF16), preferred_element_type=F32))
        kd = (k.astype(F32) * k_dec).astype(BF16)
        st_ref[...] = st * c_dec + lax.dot_general(kd, v, (((0,), (0,)), ((), ())),
                                                   preferred_element_type=F32)
        yn = y * lax.rsqrt(jnp.mean(y * y, axis=-1, keepdims=True) + RET_GN_EPS)
        o_ref[...] = (_silu(g_ref[...].astype(F32)) * yn).astype(o_ref.dtype)

    return pl.pallas_call(
        kern,
        grid=(T // seq, H, n_l),
        in_specs=[pl.BlockSpec((L, DK), lambda bi, h, l: (bi * n_l + l, h)),
                  pl.BlockSpec((L, DK), lambda bi, h, l: (bi * n_l + l, H + h)),
                  pl.BlockSpec((L, DV), lambda bi, h, l: (bi * n_l + l, h)),
                  pl.BlockSpec((L, DV), lambda bi, h, l: (bi * n_l + l, H + h))],
        out_specs=pl.BlockSpec((L, DV), lambda bi, h, l: (bi * n_l + l, h)),
        out_shape=jax.ShapeDtypeStruct((T, H * DV), BF16),
        scratch_shapes=[pltpu.VMEM((DK, DV), F32)],
        compiler_params=_cparams(3),
        name="retention_core",
    )(qk, qk, vg, vg)


def _retention_mixer(xf, xb, pos_col, seq, w_in, w_o, g, b):
    D = xf.shape[1]
    half = RET_DK // 2
    freq = (ROPE_BASE ** -jnp.linspace(0.0, 1.0, half, dtype=F32)).reshape(1, half)
    cos, sin = _rope_tables(pos_col, freq)
    n_q = D // RET_DK

    def epi_rope(accs, rows, cols, j):
        t1, t2 = accs[0][:, :half], accs[0][:, half:]
        c, s = rows
        scale = jnp.where(j >= n_q, RET_DK ** -0.5, 1.0).astype(F32)
        return (jnp.concatenate([t1 * c - t2 * s, t1 * s + t2 * c], axis=1) * scale,)

    (qk,) = _mm(xb, [(w_in, (), 0)], epi_rope, [(2 * D, BF16)], tm=1024, tn=RET_DK,
                name="ret_qk_rope", rows=[(cos, half, lambda j: 0), (sin, half, lambda j: 0)])
    (vg,) = _mm(xb, [(w_in, (), 2 * D // 512)], _epi_plain, [(4 * D, BF16)], tm=1024, tn=512,
                name="ret_vg")
    y = _retention_core(qk, vg, seq)
    return _mm_ln(y, w_o, (), xf, g, b, name="ret_out_ln")


def _moe_router(xf, w_router, *, tm=1024):
    T, D = xf.shape
    E = w_router.shape[1]
    LANES = 128
    wpad = jnp.pad(w_router, ((0, 0), (0, LANES - E)))

    def kern(x_ref, w_ref, id_ref, p_ref):
        logits = jnp.dot(x_ref[...], w_ref[...], precision=lax.Precision.HIGHEST,
                         preferred_element_type=F32)
        lane = lax.broadcasted_iota(jnp.int32, logits.shape, 1).astype(F32)
        neg = jnp.float32(-jnp.inf)
        l1 = jnp.where(lane < E, logits, neg)
        m1 = jnp.max(l1, axis=-1, keepdims=True)
        i1 = jnp.min(jnp.where(l1 == m1, lane, float(LANES)), axis=-1, keepdims=True)
        l2 = jnp.where(lane == i1, neg, l1)
        m2 = jnp.max(l2, axis=-1, keepdims=True)
        i2 = jnp.min(jnp.where(l2 == m2, lane, float(LANES)), axis=-1, keepdims=True)
        e = jnp.exp(m2 - m1)
        p1 = 1.0 / (1.0 + e)
        p2 = e / (1.0 + e)
        id_ref[...] = jnp.where(lane == 0, i1, jnp.where(lane == 1, i2, 0.0)).astype(jnp.int32)
        p_ref[...] = jnp.where(lane == 0, p1, jnp.where(lane == 1, p2, 0.0))

    out = pl.BlockSpec((tm, LANES), lambda i: (i, 0))
    return pl.pallas_call(
        kern,
        grid=(T // tm,),
        in_specs=[pl.BlockSpec((tm, D), lambda i: (i, 0)), pl.BlockSpec((D, LANES), lambda i: (0, 0))],
        out_specs=[out, out],
        out_shape=[jax.ShapeDtypeStruct((T, LANES), jnp.int32), jax.ShapeDtypeStruct((T, LANES), F32)],
        compiler_params=_cparams(1),
        name="moe_router",
    )(xf, wpad)


def _moe_plan(ids, n_tiles):
    T = ids.shape[0]
    e_flat = ids.reshape(-1)
    onehot = (e_flat[:, None] == jnp.arange(N_EXPERTS, dtype=jnp.int32)[None, :]).astype(jnp.int32)
    csum = jnp.cumsum(onehot, axis=0)
    rank = jnp.sum((csum - onehot) * onehot, axis=1)
    counts = csum[-1]
    padded = ((counts + MOE_TILE - 1) // MOE_TILE) * MOE_TILE
    ends = jnp.cumsum(padded)
    starts = ends - padded
    pos = (jnp.sum(onehot * starts[None, :], axis=1) + rank).astype(jnp.int32)
    tile_start = jnp.arange(n_tiles, dtype=jnp.int32) * MOE_TILE
    tile_e = jnp.sum((tile_start[:, None] >= ends[None, :]).astype(jnp.int32), axis=1)
    tile_e = jnp.minimum(tile_e, N_EXPERTS - 1).astype(jnp.int32)
    n_used = (ends[-1] // MOE_TILE).astype(jnp.int32).reshape(1)
    return pos.reshape(T, TOP_K), tile_e, n_used


def _moe_dispatch(xf, pos, n_rows, *, tq=256):
    T, D = xf.shape
    n_steps = T // tq
    pos_blocks = pos.reshape(n_steps, 1, tq * TOP_K)

    def kern(pos_ref, x_ref, init_ref, xs_ref, sem):
        del init_ref

        def copy(r, k):
            dst = pos_ref[0, TOP_K * r + k]
            return pltpu.make_async_copy(x_ref.at[pl.ds(r, 1)], xs_ref.at[pl.ds(dst, 1)], sem)

        def start(r, c):
            for k in range(TOP_K):
                copy(r, k).start()
            return c

        def wait(r, c):
            for k in range(TOP_K):
                copy(r, k).wait()
            return c

        lax.fori_loop(0, tq, start, 0, unroll=DMA_UNROLL)
        lax.fori_loop(0, tq, wait, 0, unroll=DMA_UNROLL)

    init = jnp.zeros((n_rows, D), F32)
    return pl.pallas_call(
        kern,
        grid=(n_steps,),
        in_specs=[pl.BlockSpec((None, 1, tq * TOP_K), lambda i: (i, 0, 0), memory_space=pltpu.SMEM),
                  pl.BlockSpec((tq, D), lambda i: (i, 0)),
                  pl.BlockSpec(memory_space=pl.ANY)],
        out_specs=pl.BlockSpec(memory_space=pl.ANY),
        out_shape=jax.ShapeDtypeStruct((n_rows, D), F32),
        scratch_shapes=[pltpu.SemaphoreType.DMA],
        input_output_aliases={2: 0},
        compiler_params=_cparams(1),
        name="moe_dispatch",
    )(pos_blocks, xf, init)


def _moe_experts(xs, tile_e, n_used, w_gu, w_down, layer):
    P, D = xs.shape
    F = w_down.shape[2]
    tm = MOE_TILE
    n_tiles = P // tm
    tn_up = 1792
    n_f = F // tn_up
    tn_dn = 1024

    def tile(i, nu):
        return jnp.minimum(i, nu[0] - 1)

    def fresh(te, nu):
        i = pl.program_id(1)
        prev = te[jnp.maximum(i - 1, 0)]
        return jnp.logical_and(i < nu[0], jnp.logical_or(i == 0, te[i] != prev))

    def up_kern(te, nu, x_ref, wg_ref, wu_ref, h_ref, wgb, wub):
        @pl.when(fresh(te, nu))
        def _():
            wgb[...] = wg_ref[...].astype(BF16)
            wub[...] = wu_ref[...].astype(BF16)

        @pl.when(pl.program_id(1) < nu[0])
        def _():
            xv = x_ref[...].astype(BF16)
            gt = jnp.dot(xv, wgb[...], preferred_element_type=F32)
            up = jnp.dot(xv, wub[...], preferred_element_type=F32)
            h_ref[...] = (_silu(gt) * up).astype(h_ref.dtype)

        @pl.when(pl.program_id(1) >= nu[0])
        def _():
            h_ref[...] = jnp.zeros_like(h_ref)

    h = pl.pallas_call(
        up_kern,
        grid_spec=pltpu.PrefetchScalarGridSpec(
            num_scalar_prefetch=2,
            grid=(n_f, n_tiles),
            in_specs=[pl.BlockSpec((tm, D), lambda f, i, te, nu: (tile(i, nu), 0)),
                      pl.BlockSpec((None, None, D, tn_up),
                                   lambda f, i, te, nu: (layer, te[tile(i, nu)], 0, f)),
                      pl.BlockSpec((None, None, D, tn_up),
                                   lambda f, i, te, nu: (layer, te[tile(i, nu)], 0, n_f + f))],
            out_specs=pl.BlockSpec((tm, tn_up), lambda f, i, te, nu: (i, f)),
            scratch_shapes=[pltpu.VMEM((D, tn_up), BF16), pltpu.VMEM((D, tn_up), BF16)]),
        out_shape=jax.ShapeDtypeStruct((P, F), BF16),
        compiler_params=_cparams(2),
        name="moe_up",
    )(tile_e, n_used, xs, w_gu, w_gu)

    def down_kern(te, nu, h_ref, wd_ref, y_ref, wdb):
        @pl.when(fresh(te, nu))
        def _():
            wdb[...] = wd_ref[...].astype(BF16)

        @pl.when(pl.program_id(1) < nu[0])
        def _():
            y_ref[...] = jnp.dot(h_ref[...], wdb[...], preferred_element_type=F32)

        @pl.when(pl.program_id(1) >= nu[0])
        def _():
            y_ref[...] = jnp.zeros_like(y_ref)

    return pl.pallas_call(
        down_kern,
        grid_spec=pltpu.PrefetchScalarGridSpec(
            num_scalar_prefetch=2,
            grid=(D // tn_dn, n_tiles),
            in_specs=[pl.BlockSpec((tm, F), lambda n, i, te, nu: (tile(i, nu), 0)),
                      pl.BlockSpec((None, None, F, tn_dn),
                                   lambda n, i, te, nu: (layer, te[tile(i, nu)], 0, n))],
            out_specs=pl.BlockSpec((tm, tn_dn), lambda n, i, te, nu: (i, n)),
            scratch_shapes=[pltpu.VMEM((F, tn_dn), BF16)]),
        out_shape=jax.ShapeDtypeStruct((P, D), F32),
        compiler_params=_cparams(2),
        name="moe_down",
    )(tile_e, n_used, h, w_down)


def _moe_combine_ln(xf, ys, pos, probs, g, b, *, tq=256):
    T, D = xf.shape
    n_steps = T // tq
    pos_blocks = pos.reshape(n_steps, 1, tq * TOP_K)

    def kern(pos_ref, x_ref, p_ref, g_ref, b_ref, ys_ref, of_ref, ob_ref, buf, sem):
        def copy(r, k):
            src = pos_ref[0, TOP_K * r + k]
            return pltpu.make_async_copy(ys_ref.at[pl.ds(src, 1)], buf.at[k, pl.ds(r, 1)], sem)

        def start(r, c):
            for k in range(TOP_K):
                copy(r, k).start()
            return c

        def wait(r, c):
            for k in range(TOP_K):
                copy(r, k).wait()
            return c

        lax.fori_loop(0, tq, start, 0, unroll=DMA_UNROLL)
        lax.fori_loop(0, tq, wait, 0, unroll=DMA_UNROLL)
        p = p_ref[...]
        z = ALPHA * x_ref[...] + p[:, 0:1] * buf[0] + p[:, 1:2] * buf[1]
        mu = jnp.mean(z, axis=-1, keepdims=True)
        d = z - mu
        var = jnp.mean(d * d, axis=-1, keepdims=True)
        y = d * lax.rsqrt(var + LN_EPS) * g_ref[...] + b_ref[...]
        of_ref[...] = y
        ob_ref[...] = y.astype(ob_ref.dtype)

    row = pl.BlockSpec((tq, D), lambda i: (i, 0))
    vec = pl.BlockSpec((1, D), lambda i: (0, 0))
    return pl.pallas_call(
        kern,
        grid=(n_steps,),
        in_specs=[pl.BlockSpec((None, 1, tq * TOP_K), lambda i: (i, 0, 0), memory_space=pltpu.SMEM),
                  row, pl.BlockSpec((tq, probs.shape[1]), lambda i: (i, 0)), vec, vec,
                  pl.BlockSpec(memory_space=pl.ANY)],
        out_specs=[row, row],
        out_shape=[jax.ShapeDtypeStruct((T, D), F32), jax.ShapeDtypeStruct((T, D), BF16)],
        scratch_shapes=[pltpu.VMEM((TOP_K, tq, D), F32), pltpu.SemaphoreType.DMA],
        compiler_params=_cparams(1),
        name="moe_combine_ln",
    )(pos_blocks, xf, probs, g.reshape(1, D), b.reshape(1, D), ys)


def _moe_ffn(xf, w_router, w_gu, w_down, layer, g, b):
    T = xf.shape[0]
    n_tiles = T * TOP_K // MOE_TILE + N_EXPERTS
    ids, probs = _moe_router(xf, w_router[layer])
    pos, tile_e, n_used = _moe_plan(ids[:, :TOP_K], n_tiles)
    xs = _moe_dispatch(xf, pos, n_tiles * MOE_TILE)
    ys = _moe_experts(xs, tile_e, n_used, w_gu, w_down, layer)
    return _moe_combine_ln(xf, ys, pos, probs, g, b)


def kernel(x, positions, ln_mix_g, ln_mix_b, ln_ffn_g, ln_ffn_b, a_w_in, a_conv_w, a_w_out, b_mix, b_w_r, b_w_k, b_w_v, b_w0, b_w1, b_w2, b_a0, b_a1, b_a2, b_g1, b_g2, b_k_k, b_k_a, b_r_k, b_gn_g, b_gn_b, b_w_o, c_w_in, c_conv_w, c_conv_b, c_w_ga, c_b_ga, c_w_gx, c_b_gx, c_lam, c_w_out, d_w_in, d_w_o, ffn_w_gu, ffn_w_down, moe_w_router, moe_w_gu, moe_w_down):
    bsz, seq, D = x.shape
    T = bsz * seq
    xf = x.reshape(T, D)
    xb = xf.astype(BF16)
    pos3 = positions.reshape(bsz, seq, 1)
    pos_col = positions.reshape(T, 1)

    xf, xb = _short_conv_mixer(xf, xb, seq, a_w_in, a_conv_w, a_w_out, ln_mix_g[0], ln_mix_b[0])
    xf, xb = _dense_ffn(xf, xb, ffn_w_gu, ffn_w_down, 0, ln_ffn_g[0], ln_ffn_b[0])

    xf, xb = _rwkv7_mixer(xf, seq, b_mix, b_w_r, b_w_k, b_w_v, b_w0, b_w1, b_w2, b_a0, b_a1, b_a2,
                          b_g1, b_g2, b_k_k, b_k_a, b_r_k, b_gn_g, b_gn_b, b_w_o,
                          ln_mix_g[1], ln_mix_b[1])
    xf, xb = _moe_ffn(xf, moe_w_router, moe_w_gu, moe_w_down, 0, ln_ffn_g[1], ln_ffn_b[1])

    xf, xb = _rglru_mixer(xf, xb, pos3, seq, c_w_in, c_conv_w, c_conv_b, c_w_ga, c_b_ga, c_w_gx,
                          c_b_gx, c_lam, c_w_out, ln_mix_g[2], ln_mix_b[2])
    xf, xb = _dense_ffn(xf, xb, ffn_w_gu, ffn_w_down, 1, ln_ffn_g[2], ln_ffn_b[2])

    xf, xb = _retention_mixer(xf, xb, pos_col, seq, d_w_in, d_w_o, ln_mix_g[3], ln_mix_b[3])
    xf, xb = _moe_ffn(xf, moe_w_router, moe_w_gu, moe_w_down, 1, ln_ffn_g[3], ln_ffn_b[3])
    return xf.reshape(bsz, seq, D)
```

```python
import functools
import math

import jax
import jax.numpy as jnp
from jax import lax
from jax.experimental import pallas as pl
from jax.experimental.pallas import tpu as pltpu

F32 = jnp.float32
BF16 = jnp.bfloat16

D_MODEL = 1024
DEPTH = 4
ALPHA = (2.0 * DEPTH) ** 0.25
LN_EPS = 1e-5

RWKV_HEAD = 64
RWKV_HEADS = D_MODEL // RWKV_HEAD
RWKV_GN_EPS = 64e-5
RWKV_CHUNK = 64
RWKV_SEQS_PER_STEP = 4

D_RNN = 1280
LRU_BLOCKS = 10
LRU_BW = D_RNN // LRU_BLOCKS
LRU_C = 8.0

RET_HEADS = 4
RET_DK = D_MODEL // RET_HEADS
RET_DV = 2 * D_MODEL // RET_HEADS
RET_CHUNK = 64
RET_SUPER = 256
ROPE_BASE = 10000.0
RET_GN_EPS = 1e-6

N_EXPERTS = 8
TOP_K = 2
MOE_TILE = 512
DMA_UNROLL = 8

VMEM_LIMIT = 56 * 1024 * 1024
F32_SUBLANES = 8


def _cparams(n_grid):
    return pltpu.CompilerParams(dimension_semantics=("arbitrary",) * n_grid,
                                vmem_limit_bytes=VMEM_LIMIT)


def _sigmoid(x):
    return 0.5 * jnp.tanh(0.5 * x) + 0.5


def _silu(x):
    return x * _sigmoid(x)


def _expm1(z):
    u = jnp.exp(z)
    safe_u = jnp.where(u == 1.0, 0.5, jnp.maximum(u, 0.5))
    small = (u - 1.0) * z / jnp.log(safe_u)
    return jnp.where(u == 1.0, z, jnp.where(u < 0.5, u - 1.0, small))


def _shift_rows(x, d, fill=0.0):
    rows = lax.broadcasted_iota(jnp.int32, x.shape, 0)
    return jnp.where(rows >= d, pltpu.roll(x, d, axis=0), fill)


def _scan_steps(a, b, axis, need_a):
    n = a.shape[axis]
    idx = lax.broadcasted_iota(jnp.int32, a.shape, axis)
    d = 1
    while d < n:
        keep = idx >= d
        b = b + a * jnp.where(keep, pltpu.roll(b, d, axis=axis), 0.0)
        if need_a or 2 * d < n:
            a = a * jnp.where(keep, pltpu.roll(a, d, axis=axis), 1.0)
        d *= 2
    return a, b


def _linear_scan_rows(a, b, a_ref, b_ref, c_ref):
    s = a.shape[0]
    n_sub = F32_SUBLANES
    g = s // n_sub
    sub = lax.broadcasted_iota(jnp.int32, a.shape, 0) % n_sub
    d = 1
    while d < n_sub:
        keep = sub >= d
        b = b + a * jnp.where(keep, pltpu.roll(b, d, axis=0), 0.0)
        a = a * jnp.where(keep, pltpu.roll(a, d, axis=0), 1.0)
        d *= 2
    a_ref[...] = a
    b_ref[...] = b
    last = pl.ds(n_sub - 1, g, stride=n_sub)
    _, tot = _scan_steps(a_ref[last, :], b_ref[last, :], 0, False)
    carry = _shift_rows(tot, 1)
    for j in range(n_sub):
        c_ref[pl.ds(j, g, stride=n_sub), :] = carry
    return b + a * c_ref[...]


def _mm(x, ws, epi, outs, *, tm, tn, name, rows=(), cols=()):
    M, K = x.shape
    n_j = outs[0][0] // tn
    n_i = M // tm
    assert M % tm == 0 and all(o[0] == n_j * tn for o in outs)
    nw = len(ws)

    in_specs = [pl.BlockSpec((tm, K), lambda j, i: (i, 0))]
    args = [x]
    w_mode = dict(pipeline_mode=pl.Buffered(1)) if n_j == 1 else {}
    for w, lead, off in ws:
        nl = len(lead)
        in_specs.append(pl.BlockSpec((None,) * nl + (K, tn),
                                     lambda j, i, lead=lead, off=off: tuple(lead) + (0, off + j),
                                     **w_mode))
        args.append(w)
    for a, width, col_fn in rows:
        in_specs.append(pl.BlockSpec((tm, width), lambda j, i, col_fn=col_fn: (i, col_fn(j))))
        args.append(a)
    for a, off in cols:
        in_specs.append(pl.BlockSpec((a.shape[0], tn), lambda j, i, off=off: (0, off + j)))
        args.append(a)

    def kern(*refs):
        x_ref = refs[0]
        w_refs = refs[1:1 + nw]
        row_refs = refs[1 + nw:1 + nw + len(rows)]
        col_refs = refs[1 + nw + len(rows):1 + nw + len(rows) + len(cols)]
        out_refs = refs[1 + nw + len(rows) + len(cols):-nw]
        wb_refs = refs[-nw:]

        @pl.when(pl.program_id(1) == 0)
        def _():
            for w_ref, wb in zip(w_refs, wb_refs):
                wb[...] = w_ref[...].astype(BF16)

        xv = x_ref[...]
        accs = [jnp.dot(xv, wb[...], preferred_element_type=F32) for wb in wb_refs]
        res = epi(accs, [r[...] for r in row_refs], [c[...] for c in col_refs], pl.program_id(0))
        for o, r in zip(out_refs, res):
            o[...] = r.astype(o.dtype)

    out = pl.pallas_call(
        kern,
        grid=(n_j, n_i),
        in_specs=in_specs,
        out_specs=[pl.BlockSpec((tm, tn), lambda j, i: (i, j)) for _ in outs],
        out_shape=[jax.ShapeDtypeStruct((M, n), dt) for n, dt in outs],
        scratch_shapes=[pltpu.VMEM((K, tn), BF16) for _ in ws],
        compiler_params=_cparams(2),
        name=name,
    )(*args)
    return out


def _epi_plain(accs, rows, cols, j):
    return (accs[0],)


def _epi_swiglu(accs, rows, cols, j):
    return (_silu(accs[0]) * accs[1],)


def _epi_ln(accs, rows, cols, j):
    z = ALPHA * rows[0] + accs[0]
    mu = jnp.mean(z, axis=-1, keepdims=True)
    d = z - mu
    var = jnp.mean(d * d, axis=-1, keepdims=True)
    y = d * lax.rsqrt(var + LN_EPS) * cols[0] + cols[1]
    return y, y


def _mm_ln(h, w, lead, xf, g, b, *, name, tm=1024):
    D = xf.shape[1]
    return _mm(h, [(w, lead, 0)], _epi_ln, [(D, F32), (D, BF16)], tm=tm, tn=D, name=name,
               rows=[(xf, D, lambda j: 0)], cols=[(g.reshape(1, D), 0), (b.reshape(1, D), 0)])


def _dense_ffn(xf, xb, w_gu, w_down, layer, g, b):
    f = w_down.shape[1]
    tn = 256
    (h,) = _mm(xb, [(w_gu, (layer,), 0), (w_gu, (layer,), f // tn)], _epi_swiglu, [(f, BF16)],
               tm=2048, tn=tn, name="ffn_up")
    return _mm_ln(h, w_down, (layer,), xf, g, b, name="ffn_down_ln")


def _short_conv_mixer(xf, xb, seq, w_in, conv_w, w_out, g, b):
    D = xf.shape[1]
    tn = 256
    nb = D // tn

    def epi(accs, rows, cols, j):
        b_gate, c_gate, v = accs
        cw = cols[0]
        cv = c_gate * v
        y = cw[2:3] * cv + cw[1:2] * _shift_rows(cv, 1) + cw[0:1] * _shift_rows(cv, 2)
        return (b_gate * y,)

    (y,) = _mm(xb, [(w_in, (), 0), (w_in, (), nb), (w_in, (), 2 * nb)], epi, [(D, BF16)],
               tm=seq, tn=tn, name="conv_in", cols=[(conv_w, 0)])
    return _mm_ln(y, w_out, (), xf, g, b, name="conv_out_ln")


def _rwkv_in(xf, seq, mix, w_r, w_k, w_v, w0, w1, w2, a0, a1, a2, g1, g2, *, tm=256):
    T, D = xf.shape
    ws = (w_r, w_k, w_v, w1, w2, a1, a2, g1, g2)
    tiles_per_seq = seq // tm
    halo = F32_SUBLANES

    def kern(x_ref, prev_ref, mix_ref, w0_ref, a0_ref, *refs):
        w_refs = refs[:len(ws)]
        r_ref, k_ref, v_ref, ld_ref, a_ref, g_ref = refs[len(ws):len(ws) + 6]
        wb = refs[len(ws) + 6:]
        i = pl.program_id(0)

        @pl.when(i == 0)
        def _():
            for w_ref, b_ref in zip(w_refs, wb):
                b_ref[...] = w_ref[...].astype(BF16)

        x = x_ref[...]
        above = jnp.where(i % tiles_per_seq == 0, 0.0, prev_ref[halo - 1:halo, :])
        rows = lax.broadcasted_iota(jnp.int32, x.shape, 0)
        xx = jnp.where(rows == 0, above, pltpu.roll(x, 1, axis=0)) - x
        m = mix_ref[...]
        xr, xw, xk, xv, xa, xg = ((x + xx * m[j:j + 1]).astype(BF16) for j in range(6))

        def dot(u, w):
            return jnp.dot(u.astype(BF16), w[...], preferred_element_type=F32)

        r_ref[...] = dot(xr, wb[0])
        k_ref[...] = dot(xk, wb[1])
        v_ref[...] = dot(xv, wb[2])
        zw = w0_ref[...] + dot(jnp.tanh(dot(xw, wb[3])), wb[4])
        w_log = -(jnp.maximum(-zw, 0.0) + jnp.log(1.0 + jnp.exp(-jnp.abs(zw)))) - 0.5
        ld_ref[...] = -jnp.exp(w_log)
        a_ref[...] = _sigmoid(a0_ref[...] + dot(dot(xa, wb[5]), wb[6]))
        g_ref[...] = dot(_sigmoid(dot(xg, wb[7])), wb[8])

    row = pl.BlockSpec((tm, D), lambda i: (i, 0))
    full = lambda a: pl.BlockSpec(a.shape, lambda i: (0, 0), pipeline_mode=pl.Buffered(1))
    vec = pl.BlockSpec((1, D), lambda i: (0, 0))
    return pl.pallas_call(
        kern,
        grid=(T // tm,),
        in_specs=[row,
                  pl.BlockSpec((halo, D), lambda i: (jnp.maximum(i * (tm // halo) - 1, 0), 0)),
                  pl.BlockSpec(mix.shape, lambda i: (0, 0)), vec, vec] + [full(w) for w in ws],
        out_specs=[row] * 6,
        out_shape=[jax.ShapeDtypeStruct((T, D), F32) for _ in range(6)],
        scratch_shapes=[pltpu.VMEM(w.shape, BF16) for w in ws],
        compiler_params=_cparams(1),
        name="rwkv_in",
    )(xf, xf, mix, w0.reshape(1, D), a0.reshape(1, D), *ws)


def _rwkv_scan(r, k, v, ld, a, g, seq, k_k, k_a, r_k, gn_g, gn_b):
    T, D = r.shape
    C = RWKV_CHUNK
    N = RWKV_HEAD
    PW = 2 * N
    NP = D // PW
    NB = RWKV_SEQS_PER_STEP
    n_c = seq // C

    def kern(r_ref, k_ref, v_ref, ld_ref, a_ref, g_ref, kk_ref, ka_ref, rk_ref, gg_ref, gb_ref,
             o_ref, st_ref):
        @pl.when(pl.program_id(1) == 0)
        def _():
            st_ref[...] = jnp.zeros_like(st_ref)

        lane = lax.broadcasted_iota(jnp.int32, (C, PW), 1)
        rowi = lax.broadcasted_iota(jnp.int32, (C, PW), 0)
        lo = lane < N
        col = jnp.where(lo, lane, lane - N)
        strict = rowi > col
        incl = rowi >= col
        lo2 = lax.broadcasted_iota(jnp.int32, (2 * C, PW), 1) < N
        kr = lax.broadcasted_iota(jnp.int32, (PW, PW), 0) < N
        kc = lax.broadcasted_iota(jnp.int32, (PW, PW), 1) < N
        same_head = kr == kc
        ones_bd = same_head.astype(BF16)
        tri = (lax.broadcasted_iota(jnp.int32, (C, C), 0)
               >= lax.broadcasted_iota(jnp.int32, (C, C), 1)).astype(F32)

        def pairs(x):
            return [x[:, p * PW:(p + 1) * PW] for p in range(NP)]

        def head_sum(x):
            xs = jnp.concatenate(pairs(x), axis=0).astype(BF16)
            s = jnp.dot(xs, ones_bd, preferred_element_type=F32)
            return jnp.concatenate([s[p * C:(p + 1) * C] for p in range(NP)], axis=1)

        at, bt, kt, rt, vp, gl, bonus = [], [], [], [], [], [], []
        for bb in range(NB):
            r_all, k_all, v_all, a_all = r_ref[bb], k_ref[bb], v_ref[bb], a_ref[bb]
            ld_all = ld_ref[bb]
            cum = jnp.dot(tri, ld_all, precision=lax.Precision.HIGHEST,
                          preferred_element_type=F32)
            kk = k_all * kk_ref[...]
            kk = kk / jnp.maximum(jnp.sqrt(head_sum(kk * kk)), 1e-12)
            km = k_all * (1.0 + (a_all - 1.0) * ka_ref[...])
            gam = jnp.exp(cum)
            gam_inv = jnp.exp(-cum)
            at += pairs(-kk * jnp.exp(cum - ld_all))
            bt += pairs(kk * a_all * gam_inv)
            kt += pairs(km * gam_inv)
            rt += pairs(r_all * gam)
            vp += pairs(v_all)
            gl += pairs(gam[C - 1:C, :])
            bonus.append(head_sum(r_all * km * rk_ref[...]) * v_all)

        def nt(x, y):
            return lax.dot_general(x, y, (((1,), (1,)), ((), ())), preferred_element_type=F32)

        def split(x):
            z = jnp.zeros_like(x)
            return jnp.concatenate([jnp.where(lo, x, z), jnp.where(lo, z, x)], axis=0).astype(BF16)

        chains = range(NB * NP)
        ar = [jnp.concatenate([at[p], rt[p]], axis=0) for p in chains]
        ar_b = [x.astype(BF16) for x in ar]
        bk = [jnp.concatenate([bt[p], kt[p]], axis=0) for p in chains]
        kb_b = [jnp.concatenate([kt[p], bt[p]], axis=0).astype(BF16) for p in chains]
        st = [st_ref[p] for p in chains]
        m0 = [nt(jnp.where(lo2, ar[p], 0.0).astype(BF16), bk[p].astype(BF16)) for p in chains]
        m1 = [nt(jnp.where(lo2, 0.0, ar[p]).astype(BF16), kb_b[p]) for p in chains]
        ars = [nt(ar_b[p], st[p].astype(BF16)) for p in chains]
        lmat = [jnp.where(strict, jnp.where(lo, m0[p][:C], m1[p][:C]), 0.0).astype(BF16)
                for p in chains]
        akm = [jnp.where(strict, jnp.where(lo, m1[p][:C], m0[p][:C]), 0.0).astype(BF16)
               for p in chains]
        v_hi_lo = [jnp.concatenate([jnp.where(lo, 0.0, vp[p]), jnp.where(lo, vp[p], 0.0)],
                                   axis=0).astype(BF16) for p in chains]
        u = [ars[p][:C] + jnp.dot(akm[p], v_hi_lo[p], preferred_element_type=F32)
             for p in chains]
        n = 1
        while n < C:
            u = [u[p] + jnp.dot(lmat[p], split(u[p]), preferred_element_type=F32)
                 for p in chains]
            n *= 2
            if n < C:
                lmat = [jnp.dot(lmat[p], split(lmat[p]),
                                preferred_element_type=F32).astype(BF16) for p in chains]
        ys = []
        for p in chains:
            rmat = jnp.concatenate([jnp.where(incl, m0[p][C:], 0.0),
                                    jnp.where(incl, m1[p][C:], 0.0)], axis=1).astype(BF16)
            u0, u1 = jnp.where(lo, u[p], 0.0), jnp.where(lo, 0.0, u[p])
            v0, v1 = jnp.where(lo, vp[p], 0.0), jnp.where(lo, 0.0, vp[p])
            uvs = jnp.concatenate([u0, v0, v1, u1], axis=0).astype(BF16)
            ys.append(ars[p][C:] + jnp.dot(rmat, uvs, preferred_element_type=F32))
            uv = jnp.concatenate([u[p], vp[p]], axis=0).astype(BF16)
            upd = lax.dot_general(uv, (bk[p] * gl[p]).astype(BF16), (((0,), (0,)), ((), ())),
                                  preferred_element_type=F32)
            st_ref[p] = st[p] * gl[p] + jnp.where(same_head, upd, 0.0)

        for bb in range(NB):
            y = jnp.concatenate(ys[bb * NP:(bb + 1) * NP], axis=1)
            dy = y - head_sum(y) * (1.0 / N)
            var = head_sum(dy * dy) * (1.0 / N)
            yn = dy * lax.rsqrt(var + RWKV_GN_EPS) * gg_ref[...] + gb_ref[...]
            o_ref[bb] = ((yn + bonus[bb]) * g_ref[bb]).astype(o_ref.dtype)

    n_seq = T // seq
    row = pl.BlockSpec((NB, C, D), lambda bi, c: (bi, c, 0))
    vec = pl.BlockSpec((1, D), lambda bi, c: (0, 0))
    seqs = [t.reshape(n_seq, seq, D) for t in (r, k, v, ld, a, g)]
    out = pl.pallas_call(
        kern,
        grid=(n_seq // NB, n_c),
        in_specs=[row] * 6 + [vec] * 5,
        out_specs=row,
        out_shape=jax.ShapeDtypeStruct((n_seq, seq, D), BF16),
        scratch_shapes=[pltpu.VMEM((NB * NP, PW, PW), F32)],
        compiler_params=_cparams(2),
        name="rwkv_scan",
    )(*seqs, k_k.reshape(1, D), k_a.reshape(1, D), r_k.reshape(1, D),
      gn_g.reshape(1, D), gn_b.reshape(1, D))
    return out.reshape(T, D)


def _rwkv7_mixer(xf, seq, mix, w_r, w_k, w_v, w0, w1, w2, a0, a1, a2, g1, g2,
                 k_k, k_a, r_k, gn_g, gn_b, w_o, g, b):
    r, k, v, ld, a, gate = _rwkv_in(xf, seq, mix, w_r, w_k, w_v, w0, w1, w2, a0, a1, a2, g1, g2)
    y = _rwkv_scan(r, k, v, ld, a, gate, seq, k_k, k_a, r_k, gn_g, gn_b)
    return _mm_ln(y, w_o, (), xf, g, b, name="rwkv_out_ln")


def _rglru_core(proj, pos3, seq, conv_w, conv_b, w_ga, b_ga, w_gx, b_gx, lam):
    T = proj.shape[0]
    W = LRU_BW
    n_conv = conv_w.shape[0]

    def kern(gate_ref, u_ref, pos_ref, cw_ref, cb_ref, wga_ref, bga_ref, wgx_ref, bgx_ref,
             lam_ref, o_ref, sa_ref, sb_ref, sc_ref, pad_ref):
        u = u_ref[...]
        cw = cw_ref[...]
        uc = cw[n_conv - 1:n_conv] * u + cb_ref[...]
        pad_ref[0:F32_SUBLANES, :] = jnp.zeros((F32_SUBLANES, W), F32)
        pad_ref[F32_SUBLANES:, :] = u
        for d in range(1, n_conv):
            uc = uc + cw[n_conv - 1 - d:n_conv - d] * pad_ref[F32_SUBLANES - d:F32_SUBLANES - d + seq, :]
        ub = uc.astype(BF16)
        rg = _sigmoid(jnp.dot(ub, wga_ref[...].astype(BF16), preferred_element_type=F32)
                      + bga_ref[...])
        ig = _sigmoid(jnp.dot(ub, wgx_ref[...].astype(BF16), preferred_element_type=F32)
                      + bgx_ref[...])
        nl = -lam_ref[...]
        softplus = jnp.maximum(nl, 0.0) + jnp.log1p(jnp.exp(-jnp.abs(nl)))
        log_a = -LRU_C * rg * softplus
        reset = pos_ref[...] == 0
        a = jnp.where(reset, 0.0, jnp.exp(log_a))
        mult = jnp.where(reset, 1.0, jnp.sqrt(-_expm1(2.0 * log_a)))
        hb = _linear_scan_rows(a, mult * (ig * uc), sa_ref, sb_ref, sc_ref)
        gt = gate_ref[...]
        gelu = 0.5 * gt * (1.0 + jnp.tanh(math.sqrt(2.0 / math.pi) * (gt + 0.044715 * gt * gt * gt)))
        o_ref[...] = (gelu * hb).astype(o_ref.dtype)

    vec = pl.BlockSpec((1, W), lambda bi, j: (0, j))
    blk = pl.BlockSpec((None, W, W), lambda bi, j: (j, 0, 0))
    return pl.pallas_call(
        kern,
        grid=(T // seq, LRU_BLOCKS),
        in_specs=[pl.BlockSpec((seq, W), lambda bi, j: (bi, j)),
                  pl.BlockSpec((seq, W), lambda bi, j: (bi, LRU_BLOCKS + j)),
                  pl.BlockSpec((None, seq, 1), lambda bi, j: (bi, 0, 0)),
                  pl.BlockSpec((n_conv, W), lambda bi, j: (0, j)),
                  vec, blk, vec, blk, vec, vec],
        out_specs=pl.BlockSpec((seq, W), lambda bi, j: (bi, j)),
        out_shape=jax.ShapeDtypeStruct((T, D_RNN), BF16),
        scratch_shapes=[pltpu.VMEM((seq, W), F32)] * 3 + [pltpu.VMEM((seq + F32_SUBLANES, W), F32)],
        compiler_params=_cparams(2),
        name="rglru_core",
    )(proj, proj, pos3, conv_w, conv_b.reshape(1, -1), w_ga, b_ga.reshape(1, -1),
      w_gx, b_gx.reshape(1, -1), lam.reshape(1, -1))


def _rglru_mixer(xf, xb, pos3, seq, w_in, conv_w, conv_b, w_ga, b_ga, w_gx, b_gx, lam, w_out, g, b):
    (proj,) = _mm(xb, [(w_in, (), 0)], _epi_plain, [(2 * D_RNN, F32)], tm=2048, tn=512,
                  name="rglru_in")
    y = _rglru_core(proj, pos3, seq, conv_w, conv_b, w_ga, b_ga, w_gx, b_gx, lam)
    return _mm_ln(y, w_out, (), xf, g, b, name="rglru_out_ln")


def _rope_tables(pos_col, freq, *, tm=2048):
    T = pos_col.shape[0]
    half = freq.shape[1]

    def kern(pos_ref, f_ref, cos_ref, sin_ref):
        ang = pos_ref[...].astype(F32) * f_ref[...]
        cos_ref[...] = jnp.cos(ang)
        sin_ref[...] = jnp.sin(ang)

    out = pl.BlockSpec((tm, half), lambda i: (i, 0))
    return pl.pallas_call(
        kern,
        grid=(T // tm,),
        in_specs=[pl.BlockSpec((tm, 1), lambda i: (i, 0)), pl.BlockSpec((1, half), lambda i: (0, 0))],
        out_specs=[out, out],
        out_shape=[jax.ShapeDtypeStruct((T, half), F32)] * 2,
        compiler_params=_cparams(1),
        name="rope_tables",
    )(pos_col, freq)


def _retention_core(qk, vg, seq):
    T = qk.shape[0]
    L, CH = RET_SUPER, RET_CHUNK
    H, DK, DV = RET_HEADS, RET_DK, RET_DV
    n_l = seq // L

    def kern(qk_ref, vg_ref, o_ref, st_ref, decay_ref, qdec_ref, kdec_ref):
        @pl.when(jnp.logical_and(pl.program_id(0) == 0, pl.program_id(1) == 0))
        def _():
            ri = lax.broadcasted_iota(jnp.int32, (L, L), 0)
            ci = lax.broadcasted_iota(jnp.int32, (L, L), 1)
            dist = jnp.abs(ri - ci).astype(F32)
            allowed = ci // CH <= ri // CH
            idx = lax.broadcasted_iota(jnp.int32, (L, 1), 0).astype(F32)
            for h in range(H):
                log_g = math.log1p(-(2.0 ** (-5.0 - h)))
                decay_ref[h] = jnp.where(allowed, jnp.exp(dist * log_g), 0.0)
                qdec_ref[h] = jnp.exp((idx + 1.0) * log_g)
                kdec_ref[h] = jnp.exp((L - 1.0 - idx) * log_g)

        @pl.when(pl.program_id(1) == 0)
        def _():
            st_ref[...] = jnp.zeros_like(st_ref)

        for h in range(H):
            q = qk_ref[:, h * DK:(h + 1) * DK]
            k = qk_ref[:, (H + h) * DK:(H + h + 1) * DK]
            v = vg_ref[:, h * DV:(h + 1) * DV]
            g = vg_ref[:, (H + h) * DV:(H + h + 1) * DV]
            st = st_ref[h]
            scores = lax.dot_general(q, k, (((1,), (1,)), ((), ())), preferred_element_type=F32)
            scores = (scores * decay_ref[h]).astype(BF16)
            qd = (q.astype(F32) * qdec_ref[h]).astype(BF16)
            y = (jnp.dot(scores, v, preferred_element_type=F32)
                 + jnp.dot(qd, st.astype(BF16), preferred_element_type=F32))
            kd = (k.astype(F32) * kdec_ref[h]).astype(BF16)
            c_dec = math.exp(L * math.log1p(-(2.0 ** (-5.0 - h))))
            st_ref[h] = st * c_dec + lax.dot_general(kd, v, (((0,), (0,)), ((), ())),
                                                     preferred_element_type=F32)
            yn = y * lax.rsqrt(jnp.mean(y * y, axis=-1, keepdims=True) + RET_GN_EPS)
            o_ref[:, h * DV:(h + 1) * DV] = (_silu(g.astype(F32)) * yn).astype(o_ref.dtype)

    return pl.pallas_call(
        kern,
        grid=(T // seq, n_l),
        in_specs=[pl.BlockSpec((L, 2 * H * DK), lambda bi, l: (bi * n_l + l, 0)),
                  pl.BlockSpec((L, 2 * H * DV), lambda bi, l: (bi * n_l + l, 0))],
        out_specs=pl.BlockSpec((L, H * DV), lambda bi, l: (bi * n_l + l, 0)),
        out_shape=jax.ShapeDtypeStruct((T, H * DV), BF16),
        scratch_shapes=[pltpu.VMEM((H, DK, DV), F32), pltpu.VMEM((H, L, L), F32),
                        pltpu.VMEM((H, L, 1), F32), pltpu.VMEM((H, L, 1), F32)],
        compiler_params=_cparams(2),
        name="retention_core",
    )(qk, vg)


def _retention_mixer(xf, xb, pos_col, seq, w_in, w_o, g, b):
    D = xf.shape[1]
    half = RET_DK // 2
    freq = (ROPE_BASE ** -jnp.linspace(0.0, 1.0, half, dtype=F32)).reshape(1, half)
    cos, sin = _rope_tables(pos_col, freq)
    n_q = D // RET_DK

    def epi_rope(accs, rows, cols, j):
        t1, t2 = accs[0][:, :half], accs[0][:, half:]
        c, s = rows
        scale = jnp.where(j >= n_q, RET_DK ** -0.5, 1.0).astype(F32)
        return (jnp.concatenate([t1 * c - t2 * s, t1 * s + t2 * c], axis=1) * scale,)

    (qk,) = _mm(xb, [(w_in, (), 0)], epi_rope, [(2 * D, BF16)], tm=2048, tn=RET_DK,
                name="ret_qk_rope", rows=[(cos, half, lambda j: 0), (sin, half, lambda j: 0)])
    (vg,) = _mm(xb, [(w_in, (), 2 * D // 512)], _epi_plain, [(4 * D, BF16)], tm=2048, tn=512,
                name="ret_vg")
    y = _retention_core(qk, vg, seq)
    return _mm_ln(y, w_o, (), xf, g, b, name="ret_out_ln")


def _moe_router(xf, w_router, *, tm=1024):
    T, D = xf.shape
    E = w_router.shape[1]
    LANES = 128
    wpad = jnp.pad(w_router, ((0, 0), (0, LANES - E)))

    def kern(x_ref, w_ref, id_ref, p_ref):
        x = x_ref[...]
        w = w_ref[...]
        x_hi = x.astype(BF16)
        x_lo = (x - x_hi.astype(F32)).astype(BF16)
        w_hi = w.astype(BF16)
        w_lo = (w - w_hi.astype(F32)).astype(BF16)
        logits = (jnp.dot(x_hi, w_hi, preferred_element_type=F32)
                  + jnp.dot(x_lo, w_hi, preferred_element_type=F32)
                  + jnp.dot(x_hi, w_lo, preferred_element_type=F32))
        lane = lax.broadcasted_iota(jnp.int32, logits.shape, 1).astype(F32)
        neg = jnp.float32(-jnp.inf)
        l1 = jnp.where(lane < E, logits, neg)
        m1 = jnp.max(l1, axis=-1, keepdims=True)
        i1 = jnp.min(jnp.where(l1 == m1, lane, float(LANES)), axis=-1, keepdims=True)
        l2 = jnp.where(lane == i1, neg, l1)
        m2 = jnp.max(l2, axis=-1, keepdims=True)
        i2 = jnp.min(jnp.where(l2 == m2, lane, float(LANES)), axis=-1, keepdims=True)
        e = jnp.exp(m2 - m1)
        p1 = 1.0 / (1.0 + e)
        p2 = e / (1.0 + e)
        id_ref[...] = jnp.where(lane == 0, i1, jnp.where(lane == 1, i2, 0.0)).astype(jnp.int32)
        p_ref[...] = jnp.where(lane == 0, p1, jnp.where(lane == 1, p2, 0.0))

    out = pl.BlockSpec((tm, LANES), lambda i: (i, 0))
    return pl.pallas_call(
        kern,
        grid=(T // tm,),
        in_specs=[pl.BlockSpec((tm, D), lambda i: (i, 0)), pl.BlockSpec((D, LANES), lambda i: (0, 0))],
        out_specs=[out, out],
        out_shape=[jax.ShapeDtypeStruct((T, LANES), jnp.int32), jax.ShapeDtypeStruct((T, LANES), F32)],
        compiler_params=_cparams(1),
        name="moe_router",
    )(xf, wpad)


def _moe_plan(ids, n_tiles):
    T = ids.shape[0]
    e_flat = ids.reshape(-1)
    onehot = (e_flat[:, None] == jnp.arange(N_EXPERTS, dtype=jnp.int32)[None, :]).astype(jnp.int32)
    csum = jnp.cumsum(onehot, axis=0)
    rank = jnp.sum((csum - onehot) * onehot, axis=1)
    counts = csum[-1]
    padded = ((counts + MOE_TILE - 1) // MOE_TILE) * MOE_TILE
    ends = jnp.cumsum(padded)
    starts = ends - padded
    pos = (jnp.sum(onehot * starts[None, :], axis=1) + rank).astype(jnp.int32)
    tile_start = jnp.arange(n_tiles, dtype=jnp.int32) * MOE_TILE
    tile_e = jnp.sum((tile_start[:, None] >= ends[None, :]).astype(jnp.int32), axis=1)
    tile_e = jnp.minimum(tile_e, N_EXPERTS - 1).astype(jnp.int32)
    n_used = (ends[-1] // MOE_TILE).astype(jnp.int32).reshape(1)
    last_tile = jnp.where(padded > 0, ends // MOE_TILE - 1, -1)
    tail = n_tiles - N_EXPERTS + jnp.arange(N_EXPERTS, dtype=jnp.int32)
    tail = jnp.where(tail >= n_used[0], tail, -1)
    fill_tiles = jnp.concatenate([last_tile, tail]).astype(jnp.int32)
    return pos.reshape(T, TOP_K), tile_e, n_used, fill_tiles


def _moe_dispatch(xf, pos, fill_tiles, n_rows, *, tq=256):
    T, D = xf.shape
    n_steps = T // tq
    pos_blocks = pos.reshape(n_steps, 1, tq * TOP_K)
    n_fill = fill_tiles.shape[0]

    def kern(fill_ref, pos_ref, x_ref, xs_ref, zero_ref, sem, zsem):
        @pl.when(pl.program_id(0) == 0)
        def _():
            zero_ref[...] = jnp.zeros_like(zero_ref)

            def fill(z):
                row0 = pl.multiple_of(fill_ref[z] * MOE_TILE, MOE_TILE)
                return pltpu.make_async_copy(zero_ref, xs_ref.at[pl.ds(row0, MOE_TILE)], zsem)

            for z in range(n_fill):
                @pl.when(fill_ref[z] >= 0)
                def _():
                    fill(z).start()
            for z in range(n_fill):
                @pl.when(fill_ref[z] >= 0)
                def _():
                    fill(z).wait()

        def copy(r, k):
            dst = pos_ref[0, TOP_K * r + k]
            return pltpu.make_async_copy(x_ref.at[pl.ds(r, 1)], xs_ref.at[pl.ds(dst, 1)], sem)

        def start(r, c):
            for k in range(TOP_K):
                copy(r, k).start(priority=k % 2)
            return c

        def wait(r, c):
            for k in range(TOP_K):
                copy(r, k).wait()
            return c

        lax.fori_loop(0, tq, start, 0, unroll=DMA_UNROLL)
        lax.fori_loop(0, tq, wait, 0, unroll=DMA_UNROLL)

    return pl.pallas_call(
        kern,
        grid_spec=pltpu.PrefetchScalarGridSpec(
            num_scalar_prefetch=1,
            grid=(n_steps,),
            in_specs=[pl.BlockSpec((None, 1, tq * TOP_K), lambda i, ft: (i, 0, 0),
                                   memory_space=pltpu.SMEM),
                      pl.BlockSpec((tq, D), lambda i, ft: (i, 0))],
            out_specs=pl.BlockSpec(memory_space=pl.ANY),
            scratch_shapes=[pltpu.VMEM((MOE_TILE, D), F32), pltpu.SemaphoreType.DMA,
                            pltpu.SemaphoreType.DMA]),
        out_shape=jax.ShapeDtypeStruct((n_rows, D), F32),
        compiler_params=_cparams(1),
        name="moe_dispatch",
    )(fill_tiles, pos_blocks, xf)


def _moe_experts(xs, tile_e, n_used, w_gu, w_down, layer):
    P, D = xs.shape
    F = w_down.shape[2]
    tm = MOE_TILE
    n_tiles = P // tm
    tn_up = 1792
    n_f = F // tn_up
    tn_dn = 1024

    def tile(i, nu):
        return jnp.minimum(i, nu[0] - 1)

    def fresh(te, nu):
        i = pl.program_id(1)
        prev = te[jnp.maximum(i - 1, 0)]
        return jnp.logical_and(i < nu[0], jnp.logical_or(i == 0, te[i] != prev))

    def up_kern(te, nu, x_ref, wg_ref, wu_ref, h_ref, wgb, wub):
        @pl.when(fresh(te, nu))
        def _():
            wgb[...] = wg_ref[...].astype(BF16)
            wub[...] = wu_ref[...].astype(BF16)

        @pl.when(pl.program_id(1) < nu[0])
        def _():
            xv = x_ref[...].astype(BF16)
            gt = jnp.dot(xv, wgb[...], preferred_element_type=F32)
            up = jnp.dot(xv, wub[...], preferred_element_type=F32)
            h_ref[...] = (_silu(gt) * up).astype(h_ref.dtype)

        @pl.when(pl.program_id(1) >= nu[0])
        def _():
            h_ref[...] = jnp.zeros_like(h_ref)

    h = pl.pallas_call(
        up_kern,
        grid_spec=pltpu.PrefetchScalarGridSpec(
            num_scalar_prefetch=2,
            grid=(n_f, n_tiles),
            in_specs=[pl.BlockSpec((tm, D), lambda f, i, te, nu: (tile(i, nu), 0)),
                      pl.BlockSpec((None, None, D, tn_up),
                                   lambda f, i, te, nu: (layer, te[tile(i, nu)], 0, f)),
                      pl.BlockSpec((None, None, D, tn_up),
                                   lambda f, i, te, nu: (layer, te[tile(i, nu)], 0, n_f + f))],
            out_specs=pl.BlockSpec((tm, tn_up), lambda f, i, te, nu: (i, f)),
            scratch_shapes=[pltpu.VMEM((D, tn_up), BF16), pltpu.VMEM((D, tn_up), BF16)]),
        out_shape=jax.ShapeDtypeStruct((P, F), BF16),
        compiler_params=_cparams(2),
        name="moe_up",
    )(tile_e, n_used, xs, w_gu, w_gu)

    def down_kern(te, nu, h_ref, wd_ref, y_ref, wdb):
        @pl.when(fresh(te, nu))
        def _():
            wdb[...] = wd_ref[...].astype(BF16)

        @pl.when(pl.program_id(1) < nu[0])
        def _():
            y_ref[...] = jnp.dot(h_ref[...], wdb[...], preferred_element_type=F32)

        @pl.when(pl.program_id(1) >= nu[0])
        def _():
            y_ref[...] = jnp.zeros_like(y_ref)

    return pl.pallas_call(
        down_kern,
        grid_spec=pltpu.PrefetchScalarGridSpec(
            num_scalar_prefetch=2,
            grid=(D // tn_dn, n_tiles),
            in_specs=[pl.BlockSpec((tm, F), lambda n, i, te, nu: (tile(i, nu), 0)),
                      pl.BlockSpec((None, None, F, tn_dn),
                                   lambda n, i, te, nu: (layer, te[tile(i, nu)], 0, n))],
            out_specs=pl.BlockSpec((tm, tn_dn), lambda n, i, te, nu: (i, n)),
            scratch_shapes=[pltpu.VMEM((F, tn_dn), BF16)]),
        out_shape=jax.ShapeDtypeStruct((P, D), F32),
        compiler_params=_cparams(2),
        name="moe_down",
    )(tile_e, n_used, h, w_down)


def _moe_combine_ln(xf, ys, pos, probs, g, b, *, tq=256):
    T, D = xf.shape
    n_steps = T // tq
    pos_blocks = pos.reshape(n_steps, 1, tq * TOP_K)

    def kern(pos_ref, x_ref, p_ref, g_ref, b_ref, ys_ref, of_ref, ob_ref, buf, sem):
        def copy(r, k):
            src = pos_ref[0, TOP_K * r + k]
            return pltpu.make_async_copy(ys_ref.at[pl.ds(src, 1)], buf.at[k, pl.ds(r, 1)], sem)

        def start(r, c):
            for k in range(TOP_K):
                copy(r, k).start(priority=k % 2)
            return c

        def wait(r, c):
            for k in range(TOP_K):
                copy(r, k).wait()
            return c

        lax.fori_loop(0, tq, start, 0, unroll=DMA_UNROLL)
        lax.fori_loop(0, tq, wait, 0, unroll=DMA_UNROLL)
        p = p_ref[...]
        z = ALPHA * x_ref[...] + p[:, 0:1] * buf[0] + p[:, 1:2] * buf[1]
        mu = jnp.mean(z, axis=-1, keepdims=True)
        d = z - mu
        var = jnp.mean(d * d, axis=-1, keepdims=True)
        y = d * lax.rsqrt(var + LN_EPS) * g_ref[...] + b_ref[...]
        of_ref[...] = y
        ob_ref[...] = y.astype(ob_ref.dtype)

    row = pl.BlockSpec((tq, D), lambda i: (i, 0))
    vec = pl.BlockSpec((1, D), lambda i: (0, 0))
    return pl.pallas_call(
        kern,
        grid=(n_steps,),
        in_specs=[pl.BlockSpec((None, 1, tq * TOP_K), lambda i: (i, 0, 0), memory_space=pltpu.SMEM),
                  row, pl.BlockSpec((tq, probs.shape[1]), lambda i: (i, 0)), vec, vec,
                  pl.BlockSpec(memory_space=pl.ANY)],
        out_specs=[row, row],
        out_shape=[jax.ShapeDtypeStruct((T, D), F32), jax.ShapeDtypeStruct((T, D), BF16)],
        scratch_shapes=[pltpu.VMEM((TOP_K, tq, D), F32), pltpu.SemaphoreType.DMA],
        compiler_params=_cparams(1),
        name="moe_combine_ln",
    )(pos_blocks, xf, probs, g.reshape(1, D), b.reshape(1, D), ys)


def _moe_ffn(xf, w_router, w_gu, w_down, layer, g, b):
    T = xf.shape[0]
    n_tiles = T * TOP_K // MOE_TILE + N_EXPERTS
    ids, probs = _moe_router(xf, w_router[layer])
    pos, tile_e, n_used, fill_tiles = _moe_plan(ids[:, :TOP_K], n_tiles)
    xs = _moe_dispatch(xf, pos, fill_tiles, n_tiles * MOE_TILE)
    ys = _moe_experts(xs, tile_e, n_used, w_gu, w_down, layer)
    return _moe_combine_ln(xf, ys, pos, probs, g, b)


def kernel(x, positions, ln_mix_g, ln_mix_b, ln_ffn_g, ln_ffn_b, a_w_in, a_conv_w, a_w_out, b_mix, b_w_r, b_w_k, b_w_v, b_w0, b_w1, b_w2, b_a0, b_a1, b_a2, b_g1, b_g2, b_k_k, b_k_a, b_r_k, b_gn_g, b_gn_b, b_w_o, c_w_in, c_conv_w, c_conv_b, c_w_ga, c_b_ga, c_w_gx, c_b_gx, c_lam, c_w_out, d_w_in, d_w_o, ffn_w_gu, ffn_w_down, moe_w_router, moe_w_gu, moe_w_down):
    bsz, seq, D = x.shape
    T = bsz * seq
    xf = x.reshape(T, D)
    xb = xf.astype(BF16)
    pos3 = positions.reshape(bsz, seq, 1)
    pos_col = positions.reshape(T, 1)

    xf, xb = _short_conv_mixer(xf, xb, seq, a_w_in, a_conv_w, a_w_out, ln_mix_g[0], ln_mix_b[0])
    xf, xb = _dense_ffn(xf, xb, ffn_w_gu, ffn_w_down, 0, ln_ffn_g[0], ln_ffn_b[0])

    xf, xb = _rwkv7_mixer(xf, seq, b_mix, b_w_r, b_w_k, b_w_v, b_w0, b_w1, b_w2, b_a0, b_a1, b_a2,
                          b_g1, b_g2, b_k_k, b_k_a, b_r_k, b_gn_g, b_gn_b, b_w_o,
                          ln_mix_g[1], ln_mix_b[1])
    xf, xb = _moe_ffn(xf, moe_w_router, moe_w_gu, moe_w_down, 0, ln_ffn_g[1], ln_ffn_b[1])

    xf, xb = _rglru_mixer(xf, xb, pos3, seq, c_w_in, c_conv_w, c_conv_b, c_w_ga, c_b_ga, c_w_gx,
                          c_b_gx, c_lam, c_w_out, ln_mix_g[2], ln_mix_b[2])
    xf, xb = _dense_ffn(xf, xb, ffn_w_gu, ffn_w_down, 1, ln_ffn_g[2], ln_ffn_b[2])

    xf, xb = _retention_mixer(xf, xb, pos_col, seq, d_w_in, d_w_o, ln_mix_g[3], ln_mix_b[3])
    xf, xb = _moe_ffn(xf, moe_w_router, moe_w_gu, moe_w_down, 1, ln_ffn_g[3], ln_ffn_b[3])
    return xf.reshape(bsz, seq, D)
```

```python
import functools
import math

import jax
import jax.numpy as jnp
from jax import lax
from jax.experimental import pallas as pl
from jax.experimental.pallas import tpu as pltpu

F32 = jnp.float32
BF16 = jnp.bfloat16

D_MODEL = 1024
DEPTH = 4
ALPHA = (2.0 * DEPTH) ** 0.25
LN_EPS = 1e-5

RWKV_HEAD = 64
RWKV_HEADS = D_MODEL // RWKV_HEAD
RWKV_GN_EPS = 64e-5
RWKV_CHUNK = 64
RWKV_SEQS_PER_STEP = 4

D_RNN = 1280
LRU_BLOCKS = 10
LRU_BW = D_RNN // LRU_BLOCKS
LRU_C = 8.0

RET_HEADS = 4
RET_DK = D_MODEL // RET_HEADS
RET_DV = 2 * D_MODEL // RET_HEADS
RET_CHUNK = 64
RET_SUPER = 256
ROPE_BASE = 10000.0
RET_GN_EPS = 1e-6

N_EXPERTS = 8
TOP_K = 2
MOE_TILE = 512
DMA_UNROLL = 8

VMEM_LIMIT = 56 * 1024 * 1024
F32_SUBLANES = 8


def _cparams(n_grid):
    return pltpu.CompilerParams(dimension_semantics=("arbitrary",) * n_grid,
                                vmem_limit_bytes=VMEM_LIMIT)


def _sigmoid(x):
    return 0.5 * jnp.tanh(0.5 * x) + 0.5


def _silu(x):
    return x * _sigmoid(x)


def _expm1(z):
    u = jnp.exp(z)
    safe_u = jnp.where(u == 1.0, 0.5, jnp.maximum(u, 0.5))
    small = (u - 1.0) * z / jnp.log(safe_u)
    return jnp.where(u == 1.0, z, jnp.where(u < 0.5, u - 1.0, small))


def _shift_rows(x, d, fill=0.0):
    rows = lax.broadcasted_iota(jnp.int32, x.shape, 0)
    return jnp.where(rows >= d, pltpu.roll(x, d, axis=0), fill)


def _scan_steps(a, b, axis, need_a):
    n = a.shape[axis]
    idx = lax.broadcasted_iota(jnp.int32, a.shape, axis)
    d = 1
    while d < n:
        keep = idx >= d
        b = b + a * jnp.where(keep, pltpu.roll(b, d, axis=axis), 0.0)
        if need_a or 2 * d < n:
            a = a * jnp.where(keep, pltpu.roll(a, d, axis=axis), 1.0)
        d *= 2
    return a, b


def _linear_scan_rows(a, b, a_ref, b_ref, c_ref):
    s = a.shape[0]
    n_sub = F32_SUBLANES
    g = s // n_sub
    sub = lax.broadcasted_iota(jnp.int32, a.shape, 0) % n_sub
    d = 1
    while d < n_sub:
        keep = sub >= d
        b = b + a * jnp.where(keep, pltpu.roll(b, d, axis=0), 0.0)
        a = a * jnp.where(keep, pltpu.roll(a, d, axis=0), 1.0)
        d *= 2
    a_ref[...] = a
    b_ref[...] = b
    last = pl.ds(n_sub - 1, g, stride=n_sub)
    _, tot = _scan_steps(a_ref[last, :], b_ref[last, :], 0, False)
    carry = _shift_rows(tot, 1)
    for j in range(n_sub):
        c_ref[pl.ds(j, g, stride=n_sub), :] = carry
    return b + a * c_ref[...]


def _mm(x, ws, epi, outs, *, tm, tn, name, rows=(), cols=()):
    M, K = x.shape
    n_j = outs[0][0] // tn
    n_i = M // tm
    assert M % tm == 0 and all(o[0] == n_j * tn for o in outs)
    nw = len(ws)

    def w_block(i, j):
        return jnp.where(i == 0, j, n_j - 1)

    in_specs = [pl.BlockSpec((tm, K), lambda i, j: (i, 0))]
    args = [x]
    w_mode = dict(pipeline_mode=pl.Buffered(1)) if n_j == 1 else {}
    for w, lead, off in ws:
        nl = len(lead)
        in_specs.append(pl.BlockSpec(
            (None,) * nl + (K, tn),
            lambda i, j, lead=lead, off=off: tuple(lead) + (0, off + w_block(i, j)), **w_mode))
        args.append(w)
    for a, width, col_fn in rows:
        in_specs.append(pl.BlockSpec((tm, width), lambda i, j, col_fn=col_fn: (i, col_fn(j))))
        args.append(a)
    for a, off in cols:
        in_specs.append(pl.BlockSpec((a.shape[0], tn), lambda i, j, off=off: (0, off + j)))
        args.append(a)

    def kern(*refs):
        x_ref = refs[0]
        w_refs = refs[1:1 + nw]
        row_refs = refs[1 + nw:1 + nw + len(rows)]
        col_refs = refs[1 + nw + len(rows):1 + nw + len(rows) + len(cols)]
        out_refs = refs[1 + nw + len(rows) + len(cols):-nw]
        wb_refs = refs[-nw:]
        j = pl.program_id(1)

        @pl.when(pl.program_id(0) == 0)
        def _():
            for w_ref, wb in zip(w_refs, wb_refs):
                wb[j] = w_ref[...].astype(BF16)

        xv = x_ref[...]
        accs = [jnp.dot(xv, wb[j], preferred_element_type=F32) for wb in wb_refs]
        res = epi(accs, [r[...] for r in row_refs], [c[...] for c in col_refs], j)
        for o, r in zip(out_refs, res):
            o[...] = r.astype(o.dtype)

    out = pl.pallas_call(
        kern,
        grid=(n_i, n_j),
        in_specs=in_specs,
        out_specs=[pl.BlockSpec((tm, tn), lambda i, j: (i, j)) for _ in outs],
        out_shape=[jax.ShapeDtypeStruct((M, n), dt) for n, dt in outs],
        scratch_shapes=[pltpu.VMEM((n_j, K, tn), BF16) for _ in ws],
        compiler_params=_cparams(2),
        name=name,
    )(*args)
    return out


def _epi_plain(accs, rows, cols, j):
    return (accs[0],)


def _epi_swiglu(accs, rows, cols, j):
    return (_silu(accs[0]) * accs[1],)


def _epi_ln(accs, rows, cols, j):
    z = ALPHA * rows[0] + accs[0]
    mu = jnp.mean(z, axis=-1, keepdims=True)
    d = z - mu
    var = jnp.mean(d * d, axis=-1, keepdims=True)
    y = d * lax.rsqrt(var + LN_EPS) * cols[0] + cols[1]
    return y, y


def _mm_ln(h, w, lead, xf, g, b, *, name, tm=1024):
    D = xf.shape[1]
    return _mm(h, [(w, lead, 0)], _epi_ln, [(D, F32), (D, BF16)], tm=tm, tn=D, name=name,
               rows=[(xf, D, lambda j: 0)], cols=[(g.reshape(1, D), 0), (b.reshape(1, D), 0)])


def _dense_ffn(xf, xb, w_gu, w_down, layer, g, b):
    f = w_down.shape[1]
    tn = 256
    (h,) = _mm(xb, [(w_gu, (layer,), 0), (w_gu, (layer,), f // tn)], _epi_swiglu, [(f, BF16)],
               tm=2048, tn=tn, name="ffn_up")
    return _mm_ln(h, w_down, (layer,), xf, g, b, name="ffn_down_ln")


def _short_conv_mixer(xf, xb, seq, w_in, conv_w, w_out, g, b):
    D = xf.shape[1]
    tn = 256
    nb = D // tn

    def epi(accs, rows, cols, j):
        b_gate, c_gate, v = accs
        cw = cols[0]
        cv = c_gate * v
        y = cw[2:3] * cv + cw[1:2] * _shift_rows(cv, 1) + cw[0:1] * _shift_rows(cv, 2)
        return (b_gate * y,)

    (y,) = _mm(xb, [(w_in, (), 0), (w_in, (), nb), (w_in, (), 2 * nb)], epi, [(D, BF16)],
               tm=seq, tn=tn, name="conv_in", cols=[(conv_w, 0)])
    return _mm_ln(y, w_out, (), xf, g, b, name="conv_out_ln")


def _rwkv_in(xf, seq, mix, w_r, w_k, w_v, w0, w1, w2, a0, a1, a2, g1, g2, *, tm=256):
    T, D = xf.shape
    ws = (w_r, w_k, w_v, w1, w2, a1, a2, g1, g2)
    tiles_per_seq = seq // tm
    halo = F32_SUBLANES

    def kern(x_ref, prev_ref, mix_ref, w0_ref, a0_ref, *refs):
        w_refs = refs[:len(ws)]
        r_ref, k_ref, v_ref, ld_ref, a_ref, g_ref = refs[len(ws):len(ws) + 6]
        wb = refs[len(ws) + 6:]
        i = pl.program_id(0)

        @pl.when(i == 0)
        def _():
            for w_ref, b_ref in zip(w_refs, wb):
                b_ref[...] = w_ref[...].astype(BF16)

        x = x_ref[...]
        above = jnp.where(i % tiles_per_seq == 0, 0.0, prev_ref[halo - 1:halo, :])
        rows = lax.broadcasted_iota(jnp.int32, x.shape, 0)
        xx = jnp.where(rows == 0, above, pltpu.roll(x, 1, axis=0)) - x
        m = mix_ref[...]
        xr, xw, xk, xv, xa, xg = ((x + xx * m[j:j + 1]).astype(BF16) for j in range(6))

        def dot(u, w):
            return jnp.dot(u.astype(BF16), w[...], preferred_element_type=F32)

        r_ref[...] = dot(xr, wb[0]).astype(r_ref.dtype)
        k_ref[...] = dot(xk, wb[1]).astype(k_ref.dtype)
        v_ref[...] = dot(xv, wb[2]).astype(v_ref.dtype)
        zw = w0_ref[...] + dot(jnp.tanh(dot(xw, wb[3])), wb[4])
        w_log = -(jnp.maximum(-zw, 0.0) + jnp.log(1.0 + jnp.exp(-jnp.abs(zw)))) - 0.5
        ld_ref[...] = -jnp.exp(w_log)
        a_ref[...] = _sigmoid(a0_ref[...] + dot(dot(xa, wb[5]), wb[6]))
        g_ref[...] = dot(_sigmoid(dot(xg, wb[7])), wb[8]).astype(g_ref.dtype)

    row = pl.BlockSpec((tm, D), lambda i: (i, 0))
    full = lambda a: pl.BlockSpec(a.shape, lambda i: (0, 0), pipeline_mode=pl.Buffered(1))
    vec = pl.BlockSpec((1, D), lambda i: (0, 0))
    return pl.pallas_call(
        kern,
        grid=(T // tm,),
        in_specs=[row,
                  pl.BlockSpec((halo, D), lambda i: (jnp.maximum(i * (tm // halo) - 1, 0), 0)),
                  pl.BlockSpec(mix.shape, lambda i: (0, 0)), vec, vec] + [full(w) for w in ws],
        out_specs=[row] * 6,
        out_shape=[jax.ShapeDtypeStruct((T, D), dt) for dt in (BF16, BF16, BF16, F32, F32, BF16)],
        scratch_shapes=[pltpu.VMEM(w.shape, BF16) for w in ws],
        compiler_params=_cparams(1),
        name="rwkv_in",
    )(xf, xf, mix, w0.reshape(1, D), a0.reshape(1, D), *ws)


def _rwkv_scan(r, k, v, ld, a, g, seq, k_k, k_a, r_k, gn_g, gn_b):
    T, D = r.shape
    C = RWKV_CHUNK
    N = RWKV_HEAD
    PW = 2 * N
    NP = D // PW
    NB = RWKV_SEQS_PER_STEP
    n_c = seq // C

    def kern(r_ref, k_ref, v_ref, ld_ref, a_ref, g_ref, kk_ref, ka_ref, rk_ref, gg_ref, gb_ref,
             o_ref, st_ref):
        @pl.when(pl.program_id(1) == 0)
        def _():
            st_ref[...] = jnp.zeros_like(st_ref)

        lane = lax.broadcasted_iota(jnp.int32, (C, PW), 1)
        rowi = lax.broadcasted_iota(jnp.int32, (C, PW), 0)
        lo = lane < N
        col = jnp.where(lo, lane, lane - N)
        strict = rowi > col
        incl = rowi >= col
        lo2 = lax.broadcasted_iota(jnp.int32, (2 * C, PW), 1) < N
        kr = lax.broadcasted_iota(jnp.int32, (PW, PW), 0) < N
        kc = lax.broadcasted_iota(jnp.int32, (PW, PW), 1) < N
        same_head = kr == kc
        ones_bd = same_head.astype(BF16)
        tri = (lax.broadcasted_iota(jnp.int32, (C, C), 0)
               >= lax.broadcasted_iota(jnp.int32, (C, C), 1)).astype(F32)

        def pairs(x):
            return [x[:, p * PW:(p + 1) * PW] for p in range(NP)]

        def head_sum(x):
            xs = jnp.concatenate(pairs(x), axis=0).astype(BF16)
            s = jnp.dot(xs, ones_bd, preferred_element_type=F32)
            return jnp.concatenate([s[p * C:(p + 1) * C] for p in range(NP)], axis=1)

        at, bt, kt, rt, vp, gl, bonus = [], [], [], [], [], [], []
        for bb in range(NB):
            r_all, k_all, v_all = (t[bb].astype(F32) for t in (r_ref, k_ref, v_ref))
            a_all = a_ref[bb]
            ld_all = ld_ref[bb]
            cum = jnp.dot(tri, ld_all, precision=lax.Precision.HIGHEST,
                          preferred_element_type=F32)
            kk = k_all * kk_ref[...]
            kk = kk / jnp.maximum(jnp.sqrt(head_sum(kk * kk)), 1e-12)
            km = k_all * (1.0 + (a_all - 1.0) * ka_ref[...])
            gam = jnp.exp(cum)
            gam_inv = jnp.exp(-cum)
            at += pairs(-kk * jnp.exp(cum - ld_all))
            bt += pairs(kk * a_all * gam_inv)
            kt += pairs(km * gam_inv)
            rt += pairs(r_all * gam)
            vp += pairs(v_all)
            gl += pairs(gam[C - 1:C, :])
            bonus.append(head_sum(r_all * km * rk_ref[...]) * v_all)

        def nt(x, y):
            return lax.dot_general(x, y, (((1,), (1,)), ((), ())), preferred_element_type=F32)

        def split(x):
            z = jnp.zeros_like(x)
            return jnp.concatenate([jnp.where(lo, x, z), jnp.where(lo, z, x)], axis=0).astype(BF16)

        chains = range(NB * NP)
        ar = [jnp.concatenate([at[p], rt[p]], axis=0) for p in chains]
        ar_b = [x.astype(BF16) for x in ar]
        bk = [jnp.concatenate([bt[p], kt[p]], axis=0) for p in chains]
        kb_b = [jnp.concatenate([kt[p], bt[p]], axis=0).astype(BF16) for p in chains]
        st = [st_ref[p] for p in chains]
        m0 = [nt(jnp.where(lo2, ar[p], 0.0).astype(BF16), bk[p].astype(BF16)) for p in chains]
        m1 = [nt(jnp.where(lo2, 0.0, ar[p]).astype(BF16), kb_b[p]) for p in chains]
        ars = [nt(ar_b[p], st[p].astype(BF16)) for p in chains]
        lmat = [jnp.where(strict, jnp.where(lo, m0[p][:C], m1[p][:C]), 0.0).astype(BF16)
                for p in chains]
        akm = [jnp.where(strict, jnp.where(lo, m1[p][:C], m0[p][:C]), 0.0).astype(BF16)
               for p in chains]
        v_hi_lo = [jnp.concatenate([jnp.where(lo, 0.0, vp[p]), jnp.where(lo, vp[p], 0.0)],
                                   axis=0).astype(BF16) for p in chains]
        u = [ars[p][:C] + jnp.dot(akm[p], v_hi_lo[p], preferred_element_type=F32)
             for p in chains]
        n = 1
        while n < C:
            u = [u[p] + jnp.dot(lmat[p], split(u[p]), preferred_element_type=F32)
                 for p in chains]
            n *= 2
            if n < C:
                lmat = [jnp.dot(lmat[p], split(lmat[p]),
                                preferred_element_type=F32).astype(BF16) for p in chains]
        ys = []
        for p in chains:
            rmat = jnp.concatenate([jnp.where(incl, m0[p][C:], 0.0),
                                    jnp.where(incl, m1[p][C:], 0.0)], axis=1).astype(BF16)
            u0, u1 = jnp.where(lo, u[p], 0.0), jnp.where(lo, 0.0, u[p])
            v0, v1 = jnp.where(lo, vp[p], 0.0), jnp.where(lo, 0.0, vp[p])
            uvs = jnp.concatenate([u0, v0, v1, u1], axis=0).astype(BF16)
            ys.append(ars[p][C:] + jnp.dot(rmat, uvs, preferred_element_type=F32))
            uv = jnp.concatenate([u[p], vp[p]], axis=0).astype(BF16)
            upd = lax.dot_general(uv, (bk[p] * gl[p]).astype(BF16), (((0,), (0,)), ((), ())),
                                  preferred_element_type=F32)
            st_ref[p] = st[p] * gl[p] + jnp.where(same_head, upd, 0.0)

        for bb in range(NB):
            y = jnp.concatenate(ys[bb * NP:(bb + 1) * NP], axis=1)
            dy = y - head_sum(y) * (1.0 / N)
            var = head_sum(dy * dy) * (1.0 / N)
            yn = dy * lax.rsqrt(var + RWKV_GN_EPS) * gg_ref[...] + gb_ref[...]
            o_ref[bb] = ((yn + bonus[bb]) * g_ref[bb].astype(F32)).astype(o_ref.dtype)

    n_seq = T // seq
    row = pl.BlockSpec((NB, C, D), lambda bi, c: (bi, c, 0))
    vec = pl.BlockSpec((1, D), lambda bi, c: (0, 0))
    seqs = [t.reshape(n_seq, seq, D) for t in (r, k, v, ld, a, g)]
    out = pl.pallas_call(
        kern,
        grid=(n_seq // NB, n_c),
        in_specs=[row] * 6 + [vec] * 5,
        out_specs=row,
        out_shape=jax.ShapeDtypeStruct((n_seq, seq, D), BF16),
        scratch_shapes=[pltpu.VMEM((NB * NP, PW, PW), F32)],
        compiler_params=_cparams(2),
        name="rwkv_scan",
    )(*seqs, k_k.reshape(1, D), k_a.reshape(1, D), r_k.reshape(1, D),
      gn_g.reshape(1, D), gn_b.reshape(1, D))
    return out.reshape(T, D)


def _rwkv7_mixer(xf, seq, mix, w_r, w_k, w_v, w0, w1, w2, a0, a1, a2, g1, g2,
                 k_k, k_a, r_k, gn_g, gn_b, w_o, g, b):
    r, k, v, ld, a, gate = _rwkv_in(xf, seq, mix, w_r, w_k, w_v, w0, w1, w2, a0, a1, a2, g1, g2)
    y = _rwkv_scan(r, k, v, ld, a, gate, seq, k_k, k_a, r_k, gn_g, gn_b)
    return _mm_ln(y, w_o, (), xf, g, b, name="rwkv_out_ln")


def _rglru_core(proj, pos3, seq, conv_w, conv_b, w_ga, b_ga, w_gx, b_gx, lam):
    T = proj.shape[0]
    W = LRU_BW
    n_conv = conv_w.shape[0]

    def kern(gate_ref, u_ref, pos_ref, cw_ref, cb_ref, wga_ref, bga_ref, wgx_ref, bgx_ref,
             lam_ref, o_ref, sa_ref, sb_ref, sc_ref, pad_ref):
        u = u_ref[...]
        cw = cw_ref[...]
        uc = cw[n_conv - 1:n_conv] * u + cb_ref[...]
        pad_ref[0:F32_SUBLANES, :] = jnp.zeros((F32_SUBLANES, W), F32)
        pad_ref[F32_SUBLANES:, :] = u
        for d in range(1, n_conv):
            uc = uc + cw[n_conv - 1 - d:n_conv - d] * pad_ref[F32_SUBLANES - d:F32_SUBLANES - d + seq, :]
        ub = uc.astype(BF16)
        rg = _sigmoid(jnp.dot(ub, wga_ref[...].astype(BF16), preferred_element_type=F32)
                      + bga_ref[...])
        ig = _sigmoid(jnp.dot(ub, wgx_ref[...].astype(BF16), preferred_element_type=F32)
                      + bgx_ref[...])
        nl = -lam_ref[...]
        softplus = jnp.maximum(nl, 0.0) + jnp.log1p(jnp.exp(-jnp.abs(nl)))
        log_a = -LRU_C * rg * softplus
        reset = pos_ref[...] == 0
        a = jnp.where(reset, 0.0, jnp.exp(log_a))
        mult = jnp.where(reset, 1.0, jnp.sqrt(-_expm1(2.0 * log_a)))
        hb = _linear_scan_rows(a, mult * (ig * uc), sa_ref, sb_ref, sc_ref)
        gt = gate_ref[...]
        gelu = 0.5 * gt * (1.0 + jnp.tanh(math.sqrt(2.0 / math.pi) * (gt + 0.044715 * gt * gt * gt)))
        o_ref[...] = (gelu * hb).astype(o_ref.dtype)

    vec = pl.BlockSpec((1, W), lambda bi, j: (0, j))
    blk = pl.BlockSpec((None, W, W), lambda bi, j: (j, 0, 0))
    return pl.pallas_call(
        kern,
        grid=(T // seq, LRU_BLOCKS),
        in_specs=[pl.BlockSpec((seq, W), lambda bi, j: (bi, j)),
                  pl.BlockSpec((seq, W), lambda bi, j: (bi, LRU_BLOCKS + j)),
                  pl.BlockSpec((None, seq, 1), lambda bi, j: (bi, 0, 0)),
                  pl.BlockSpec((n_conv, W), lambda bi, j: (0, j)),
                  vec, blk, vec, blk, vec, vec],
        out_specs=pl.BlockSpec((seq, W), lambda bi, j: (bi, j)),
        out_shape=jax.ShapeDtypeStruct((T, D_RNN), BF16),
        scratch_shapes=[pltpu.VMEM((seq, W), F32)] * 3 + [pltpu.VMEM((seq + F32_SUBLANES, W), F32)],
        compiler_params=_cparams(2),
        name="rglru_core",
    )(proj, proj, pos3, conv_w, conv_b.reshape(1, -1), w_ga, b_ga.reshape(1, -1),
      w_gx, b_gx.reshape(1, -1), lam.reshape(1, -1))


def _rglru_mixer(xf, xb, pos3, seq, w_in, conv_w, conv_b, w_ga, b_ga, w_gx, b_gx, lam, w_out, g, b):
    (proj,) = _mm(xb, [(w_in, (), 0)], _epi_plain, [(2 * D_RNN, F32)], tm=2048, tn=512,
                  name="rglru_in")
    y = _rglru_core(proj, pos3, seq, conv_w, conv_b, w_ga, b_ga, w_gx, b_gx, lam)
    return _mm_ln(y, w_out, (), xf, g, b, name="rglru_out_ln")


def _rope_tables(pos_col, freq, *, tm=2048):
    T = pos_col.shape[0]
    half = freq.shape[1]

    def kern(pos_ref, f_ref, cos_ref, sin_ref):
        ang = pos_ref[...].astype(F32) * f_ref[...]
        cos_ref[...] = jnp.cos(ang)
        sin_ref[...] = jnp.sin(ang)

    out = pl.BlockSpec((tm, half), lambda i: (i, 0))
    return pl.pallas_call(
        kern,
        grid=(T // tm,),
        in_specs=[pl.BlockSpec((tm, 1), lambda i: (i, 0)), pl.BlockSpec((1, half), lambda i: (0, 0))],
        out_specs=[out, out],
        out_shape=[jax.ShapeDtypeStruct((T, half), F32)] * 2,
        compiler_params=_cparams(1),
        name="rope_tables",
    )(pos_col, freq)


def _retention_core(qk, vg, seq):
    T = qk.shape[0]
    L, CH = RET_SUPER, RET_CHUNK
    H, DK, DV = RET_HEADS, RET_DK, RET_DV
    n_l = seq // L

    def kern(qk_ref, vg_ref, o_ref, st_ref, decay_ref, qdec_ref, kdec_ref):
        @pl.when(jnp.logical_and(pl.program_id(0) == 0, pl.program_id(1) == 0))
        def _():
            ri = lax.broadcasted_iota(jnp.int32, (L, L), 0)
            ci = lax.broadcasted_iota(jnp.int32, (L, L), 1)
            dist = jnp.abs(ri - ci).astype(F32)
            allowed = ci // CH <= ri // CH
            idx = lax.broadcasted_iota(jnp.int32, (L, 1), 0).astype(F32)
            for h in range(H):
                log_g = math.log1p(-(2.0 ** (-5.0 - h)))
                decay_ref[h] = jnp.where(allowed, jnp.exp(dist * log_g), 0.0)
                qdec_ref[h] = jnp.exp((idx + 1.0) * log_g)
                kdec_ref[h] = jnp.exp((L - 1.0 - idx) * log_g)

        @pl.when(pl.program_id(1) == 0)
        def _():
            st_ref[...] = jnp.zeros_like(st_ref)

        for h in range(H):
            q = qk_ref[:, h * DK:(h + 1) * DK]
            k = qk_ref[:, (H + h) * DK:(H + h + 1) * DK]
            v = vg_ref[:, h * DV:(h + 1) * DV]
            g = vg_ref[:, (H + h) * DV:(H + h + 1) * DV]
            st = st_ref[h]
            scores = lax.dot_general(q, k, (((1,), (1,)), ((), ())), preferred_element_type=F32)
            scores = (scores * decay_ref[h]).astype(BF16)
            qd = (q.astype(F32) * qdec_ref[h]).astype(BF16)
            y = (jnp.dot(scores, v, preferred_element_type=F32)
                 + jnp.dot(qd, st.astype(BF16), preferred_element_type=F32))
            kd = (k.astype(F32) * kdec_ref[h]).astype(BF16)
            c_dec = math.exp(L * math.log1p(-(2.0 ** (-5.0 - h))))
            st_ref[h] = st * c_dec + lax.dot_general(kd, v, (((0,), (0,)), ((), ())),
                                                     preferred_element_type=F32)
            yn = y * lax.rsqrt(jnp.mean(y * y, axis=-1, keepdims=True) + RET_GN_EPS)
            o_ref[:, h * DV:(h + 1) * DV] = (_silu(g.astype(F32)) * yn).astype(o_ref.dtype)

    return pl.pallas_call(
        kern,
        grid=(T // seq, n_l),
        in_specs=[pl.BlockSpec((L, 2 * H * DK), lambda bi, l: (bi * n_l + l, 0)),
                  pl.BlockSpec((L, 2 * H * DV), lambda bi, l: (bi * n_l + l, 0))],
        out_specs=pl.BlockSpec((L, H * DV), lambda bi, l: (bi * n_l + l, 0)),
        out_shape=jax.ShapeDtypeStruct((T, H * DV), BF16),
        scratch_shapes=[pltpu.VMEM((H, DK, DV), F32), pltpu.VMEM((H, L, L), F32),
                        pltpu.VMEM((H, L, 1), F32), pltpu.VMEM((H, L, 1), F32)],
        compiler_params=_cparams(2),
        name="retention_core",
    )(qk, vg)


def _retention_mixer(xf, xb, pos_col, seq, w_in, w_o, g, b):
    D = xf.shape[1]
    half = RET_DK // 2
    freq = (ROPE_BASE ** -jnp.linspace(0.0, 1.0, half, dtype=F32)).reshape(1, half)
    cos, sin = _rope_tables(pos_col, freq)
    n_q = D // RET_DK

    def epi_rope(accs, rows, cols, j):
        t1, t2 = accs[0][:, :half], accs[0][:, half:]
        c, s = rows
        scale = jnp.where(j >= n_q, RET_DK ** -0.5, 1.0).astype(F32)
        return (jnp.concatenate([t1 * c - t2 * s, t1 * s + t2 * c], axis=1) * scale,)

    (qk,) = _mm(xb, [(w_in, (), 0)], epi_rope, [(2 * D, BF16)], tm=2048, tn=RET_DK,
                name="ret_qk_rope", rows=[(cos, half, lambda j: 0), (sin, half, lambda j: 0)])
    (vg,) = _mm(xb, [(w_in, (), 2 * D // 512)], _epi_plain, [(4 * D, BF16)], tm=2048, tn=512,
                name="ret_vg")
    y = _retention_core(qk, vg, seq)
    return _mm_ln(y, w_o, (), xf, g, b, name="ret_out_ln")


def _moe_router(xf, w_router, *, tm=1024):
    T, D = xf.shape
    E = w_router.shape[1]
    LANES = 128
    wpad = jnp.pad(w_router, ((0, 0), (0, LANES - E)))

    def kern(x_ref, w_ref, id_ref, p_ref):
        x = x_ref[...]
        w = w_ref[...]
        x_hi = x.astype(BF16)
        x_lo = (x - x_hi.astype(F32)).astype(BF16)
        w_hi = w.astype(BF16)
        w_lo = (w - w_hi.astype(F32)).astype(BF16)
        logits = (jnp.dot(x_hi, w_hi, preferred_element_type=F32)
                  + jnp.dot(x_lo, w_hi, preferred_element_type=F32)
                  + jnp.dot(x_hi, w_lo, preferred_element_type=F32))
        lane = lax.broadcasted_iota(jnp.int32, logits.shape, 1).astype(F32)
        neg = jnp.float32(-jnp.inf)
        l1 = jnp.where(lane < E, logits, neg)
        m1 = jnp.max(l1, axis=-1, keepdims=True)
        i1 = jnp.min(jnp.where(l1 == m1, lane, float(LANES)), axis=-1, keepdims=True)
        l2 = jnp.where(lane == i1, neg, l1)
        m2 = jnp.max(l2, axis=-1, keepdims=True)
        i2 = jnp.min(jnp.where(l2 == m2, lane, float(LANES)), axis=-1, keepdims=True)
        e = jnp.exp(m2 - m1)
        p1 = 1.0 / (1.0 + e)
        p2 = e / (1.0 + e)
        id_ref[...] = jnp.where(lane == 0, i1, jnp.where(lane == 1, i2, 0.0)).astype(jnp.int32)
        p_ref[...] = jnp.where(lane == 0, p1, jnp.where(lane == 1, p2, 0.0))

    out = pl.BlockSpec((tm, LANES), lambda i: (i, 0))
    return pl.pallas_call(
        kern,
        grid=(T // tm,),
        in_specs=[pl.BlockSpec((tm, D), lambda i: (i, 0)), pl.BlockSpec((D, LANES), lambda i: (0, 0))],
        out_specs=[out, out],
        out_shape=[jax.ShapeDtypeStruct((T, LANES), jnp.int32), jax.ShapeDtypeStruct((T, LANES), F32)],
        compiler_params=_cparams(1),
        name="moe_router",
    )(xf, wpad)


def _moe_plan(ids, n_tiles):
    T = ids.shape[0]
    e_flat = ids.reshape(-1)
    onehot = (e_flat[:, None] == jnp.arange(N_EXPERTS, dtype=jnp.int32)[None, :]).astype(jnp.int32)
    csum = jnp.cumsum(onehot, axis=0)
    rank = jnp.sum((csum - onehot) * onehot, axis=1)
    counts = csum[-1]
    padded = ((counts + MOE_TILE - 1) // MOE_TILE) * MOE_TILE
    ends = jnp.cumsum(padded)
    starts = ends - padded
    pos = (jnp.sum(onehot * starts[None, :], axis=1) + rank).astype(jnp.int32)
    tile_start = jnp.arange(n_tiles, dtype=jnp.int32) * MOE_TILE
    tile_e = jnp.sum((tile_start[:, None] >= ends[None, :]).astype(jnp.int32), axis=1)
    tile_e = jnp.minimum(tile_e, N_EXPERTS - 1).astype(jnp.int32)
    n_used = (ends[-1] // MOE_TILE).astype(jnp.int32).reshape(1)
    last_tile = jnp.where(padded > 0, ends // MOE_TILE - 1, -1)
    tail = n_tiles - N_EXPERTS + jnp.arange(N_EXPERTS, dtype=jnp.int32)
    tail = jnp.where(tail >= n_used[0], tail, -1)
    fill_tiles = jnp.concatenate([last_tile, tail]).astype(jnp.int32)
    return pos.reshape(T, TOP_K), tile_e, n_used, fill_tiles


def _moe_dispatch(xf, pos, fill_tiles, n_rows, *, tq=1024):
    T, D = xf.shape
    n_steps = T // tq
    pos_blocks = pos.reshape(n_steps, 1, tq * TOP_K)
    n_fill = fill_tiles.shape[0]

    def kern(fill_ref, pos_ref, x_ref, xs_ref, zero_ref, sem, zsem):
        @pl.when(pl.program_id(0) == 0)
        def _():
            zero_ref[...] = jnp.zeros_like(zero_ref)

            def fill(z):
                row0 = pl.multiple_of(fill_ref[z] * MOE_TILE, MOE_TILE)
                return pltpu.make_async_copy(zero_ref, xs_ref.at[pl.ds(row0, MOE_TILE)], zsem)

            for z in range(n_fill):
                @pl.when(fill_ref[z] >= 0)
                def _():
                    fill(z).start()
            for z in range(n_fill):
                @pl.when(fill_ref[z] >= 0)
                def _():
                    fill(z).wait()

        def copy(r, k):
            dst = pos_ref[0, TOP_K * r + k]
            return pltpu.make_async_copy(x_ref.at[pl.ds(r, 1)], xs_ref.at[pl.ds(dst, 1)], sem)

        def start(r, c):
            for k in range(TOP_K):
                copy(r, k).start(priority=k % 2)
            return c

        def wait(r, c):
            for k in range(TOP_K):
                copy(r, k).wait()
            return c

        lax.fori_loop(0, tq, start, 0, unroll=DMA_UNROLL)
        lax.fori_loop(0, tq, wait, 0, unroll=DMA_UNROLL)

    return pl.pallas_call(
        kern,
        grid_spec=pltpu.PrefetchScalarGridSpec(
            num_scalar_prefetch=1,
            grid=(n_steps,),
            in_specs=[pl.BlockSpec((None, 1, tq * TOP_K), lambda i, ft: (i, 0, 0),
                                   memory_space=pltpu.SMEM),
                      pl.BlockSpec((tq, D), lambda i, ft: (i, 0))],
            out_specs=pl.BlockSpec(memory_space=pl.ANY),
            scratch_shapes=[pltpu.VMEM((MOE_TILE, D), F32), pltpu.SemaphoreType.DMA,
                            pltpu.SemaphoreType.DMA]),
        out_shape=jax.ShapeDtypeStruct((n_rows, D), F32),
        compiler_params=_cparams(1),
        name="moe_dispatch",
    )(fill_tiles, pos_blocks, xf)


def _moe_experts(xs, tile_e, n_used, w_gu, w_down, layer):
    P, D = xs.shape
    F = w_down.shape[2]
    tm = MOE_TILE
    n_tiles = P // tm
    tn_up = 1792
    n_f = F // tn_up
    tn_dn = 1024

    def tile(i, nu):
        return jnp.minimum(i, nu[0] - 1)

    def fresh(te, nu):
        i = pl.program_id(1)
        prev = te[jnp.maximum(i - 1, 0)]
        return jnp.logical_and(i < nu[0], jnp.logical_or(i == 0, te[i] != prev))

    def up_kern(te, nu, x_ref, wg_ref, wu_ref, h_ref, wgb, wub):
        @pl.when(fresh(te, nu))
        def _():
            wgb[...] = wg_ref[...].astype(BF16)
            wub[...] = wu_ref[...].astype(BF16)

        @pl.when(pl.program_id(1) < nu[0])
        def _():
            xv = x_ref[...].astype(BF16)
            gt = jnp.dot(xv, wgb[...], preferred_element_type=F32)
            up = jnp.dot(xv, wub[...], preferred_element_type=F32)
            h_ref[...] = (_silu(gt) * up).astype(h_ref.dtype)

        @pl.when(pl.program_id(1) >= nu[0])
        def _():
            h_ref[...] = jnp.zeros_like(h_ref)

    h = pl.pallas_call(
        up_kern,
        grid_spec=pltpu.PrefetchScalarGridSpec(
            num_scalar_prefetch=2,
            grid=(n_f, n_tiles),
            in_specs=[pl.BlockSpec((tm, D), lambda f, i, te, nu: (tile(i, nu), 0)),
                      pl.BlockSpec((None, None, D, tn_up),
                                   lambda f, i, te, nu: (layer, te[tile(i, nu)], 0, f)),
                      pl.BlockSpec((None, None, D, tn_up),
                                   lambda f, i, te, nu: (layer, te[tile(i, nu)], 0, n_f + f))],
            out_specs=pl.BlockSpec((tm, tn_up), lambda f, i, te, nu: (i, f)),
            scratch_shapes=[pltpu.VMEM((D, tn_up), BF16), pltpu.VMEM((D, tn_up), BF16)]),
        out_shape=jax.ShapeDtypeStruct((P, F), BF16),
        compiler_params=_cparams(2),
        name="moe_up",
    )(tile_e, n_used, xs, w_gu, w_gu)

    def down_kern(te, nu, h_ref, wd_ref, y_ref, wdb):
        @pl.when(fresh(te, nu))
        def _():
            wdb[...] = wd_ref[...].astype(BF16)

        @pl.when(pl.program_id(1) < nu[0])
        def _():
            y_ref[...] = jnp.dot(h_ref[...], wdb[...], preferred_element_type=F32)

        @pl.when(pl.program_id(1) >= nu[0])
        def _():
            y_ref[...] = jnp.zeros_like(y_ref)

    return pl.pallas_call(
        down_kern,
        grid_spec=pltpu.PrefetchScalarGridSpec(
            num_scalar_prefetch=2,
            grid=(D // tn_dn, n_tiles),
            in_specs=[pl.BlockSpec((tm, F), lambda n, i, te, nu: (tile(i, nu), 0)),
                      pl.BlockSpec((None, None, F, tn_dn),
                                   lambda n, i, te, nu: (layer, te[tile(i, nu)], 0, n))],
            out_specs=pl.BlockSpec((tm, tn_dn), lambda n, i, te, nu: (i, n)),
            scratch_shapes=[pltpu.VMEM((F, tn_dn), BF16)]),
        out_shape=jax.ShapeDtypeStruct((P, D), F32),
        compiler_params=_cparams(2),
        name="moe_down",
    )(tile_e, n_used, h, w_down)


def _moe_combine_ln(xf, ys, pos, probs, g, b, *, tq=1024):
    T, D = xf.shape
    n_steps = T // tq
    pos_blocks = pos.reshape(n_steps, 1, tq * TOP_K)

    def kern(pos_ref, x_ref, p_ref, g_ref, b_ref, ys_ref, of_ref, ob_ref, buf, sem):
        def copy(r, k):
            src = pos_ref[0, TOP_K * r + k]
            return pltpu.make_async_copy(ys_ref.at[pl.ds(src, 1)], buf.at[k, pl.ds(r, 1)], sem)

        def start(r, c):
            for k in range(TOP_K):
                copy(r, k).start(priority=k % 2)
            return c

        def wait(r, c):
            for k in range(TOP_K):
                copy(r, k).wait()
            return c

        lax.fori_loop(0, tq, start, 0, unroll=DMA_UNROLL)
        lax.fori_loop(0, tq, wait, 0, unroll=DMA_UNROLL)
        p = p_ref[...]
        z = ALPHA * x_ref[...] + p[:, 0:1] * buf[0] + p[:, 1:2] * buf[1]
        mu = jnp.mean(z, axis=-1, keepdims=True)
        d = z - mu
        var = jnp.mean(d * d, axis=-1, keepdims=True)
        y = d * lax.rsqrt(var + LN_EPS) * g_ref[...] + b_ref[...]
        of_ref[...] = y
        ob_ref[...] = y.astype(ob_ref.dtype)

    row = pl.BlockSpec((tq, D), lambda i: (i, 0))
    vec = pl.BlockSpec((1, D), lambda i: (0, 0))
    return pl.pallas_call(
        kern,
        grid=(n_steps,),
        in_specs=[pl.BlockSpec((None, 1, tq * TOP_K), lambda i: (i, 0, 0), memory_space=pltpu.SMEM),
                  row, pl.BlockSpec((tq, probs.shape[1]), lambda i: (i, 0)), vec, vec,
                  pl.BlockSpec(memory_space=pl.ANY)],
        out_specs=[row, row],
        out_shape=[jax.ShapeDtypeStruct((T, D), F32), jax.ShapeDtypeStruct((T, D), BF16)],
        scratch_shapes=[pltpu.VMEM((TOP_K, tq, D), F32), pltpu.SemaphoreType.DMA],
        compiler_params=_cparams(1),
        name="moe_combine_ln",
    )(pos_blocks, xf, probs, g.reshape(1, D), b.reshape(1, D), ys)


def _moe_ffn(xf, w_router, w_gu, w_down, layer, g, b):
    T = xf.shape[0]
    n_tiles = T * TOP_K // MOE_TILE + N_EXPERTS
    ids, probs = _moe_router(xf, w_router[layer])
    pos, tile_e, n_used, fill_tiles = _moe_plan(ids[:, :TOP_K], n_tiles)
    xs = _moe_dispatch(xf, pos, fill_tiles, n_tiles * MOE_TILE)
    ys = _moe_experts(xs, tile_e, n_used, w_gu, w_down, layer)
    return _moe_combine_ln(xf, ys, pos, probs, g, b)


def kernel(x, positions, ln_mix_g, ln_mix_b, ln_ffn_g, ln_ffn_b, a_w_in, a_conv_w, a_w_out, b_mix, b_w_r, b_w_k, b_w_v, b_w0, b_w1, b_w2, b_a0, b_a1, b_a2, b_g1, b_g2, b_k_k, b_k_a, b_r_k, b_gn_g, b_gn_b, b_w_o, c_w_in, c_conv_w, c_conv_b, c_w_ga, c_b_ga, c_w_gx, c_b_gx, c_lam, c_w_out, d_w_in, d_w_o, ffn_w_gu, ffn_w_down, moe_w_router, moe_w_gu, moe_w_down):
    bsz, seq, D = x.shape
    T = bsz * seq
    xf = x.reshape(T, D)
    xb = xf.astype(BF16)
    pos3 = positions.reshape(bsz, seq, 1)
    pos_col = positions.reshape(T, 1)

    xf, xb = _short_conv_mixer(xf, xb, seq, a_w_in, a_conv_w, a_w_out, ln_mix_g[0], ln_mix_b[0])
    xf, xb = _dense_ffn(xf, xb, ffn_w_gu, ffn_w_down, 0, ln_ffn_g[0], ln_ffn_b[0])

    xf, xb = _rwkv7_mixer(xf, seq, b_mix, b_w_r, b_w_k, b_w_v, b_w0, b_w1, b_w2, b_a0, b_a1, b_a2,
                          b_g1, b_g2, b_k_k, b_k_a, b_r_k, b_gn_g, b_gn_b, b_w_o,
                          ln_mix_g[1], ln_mix_b[1])
    xf, xb = _moe_ffn(xf, moe_w_router, moe_w_gu, moe_w_down, 0, ln_ffn_g[1], ln_ffn_b[1])

    xf, xb = _rglru_mixer(xf, xb, pos3, seq, c_w_in, c_conv_w, c_conv_b, c_w_ga, c_b_ga, c_w_gx,
                          c_b_gx, c_lam, c_w_out, ln_mix_g[2], ln_mix_b[2])
    xf, xb = _dense_ffn(xf, xb, ffn_w_gu, ffn_w_down, 1, ln_ffn_g[2], ln_ffn_b[2])

    xf, xb = _retention_mixer(xf, xb, pos_col, seq, d_w_in, d_w_o, ln_mix_g[3], ln_mix_b[3])
    xf, xb = _moe_ffn(xf, moe_w_router, moe_w_gu, moe_w_down, 1, ln_ffn_g[3], ln_ffn_b[3])
    return xf.reshape(bsz, seq, D)
```

```python
import functools
import math

import jax
import jax.numpy as jnp
from jax import lax
from jax.experimental import pallas as pl
from jax.experimental.pallas import tpu as pltpu

F32 = jnp.float32
BF16 = jnp.bfloat16

D_MODEL = 1024
DEPTH = 4
ALPHA = (2.0 * DEPTH) ** 0.25
LN_EPS = 1e-5

RWKV_HEAD = 64
RWKV_HEADS = D_MODEL // RWKV_HEAD
RWKV_GN_EPS = 64e-5
RWKV_CHUNK = 64
RWKV_SEQS_PER_STEP = 4

D_RNN = 1280
LRU_BLOCKS = 10
LRU_BW = D_RNN // LRU_BLOCKS
LRU_C = 8.0

RET_HEADS = 4
RET_DK = D_MODEL // RET_HEADS
RET_DV = 2 * D_MODEL // RET_HEADS
RET_CHUNK = 64
RET_SUPER = 256
ROPE_BASE = 10000.0
RET_GN_EPS = 1e-6

N_EXPERTS = 8
TOP_K = 2
MOE_TILE = 512
DMA_UNROLL = 8

VMEM_LIMIT = 56 * 1024 * 1024
F32_SUBLANES = 8


def _cparams(n_grid):
    return pltpu.CompilerParams(dimension_semantics=("arbitrary",) * n_grid,
                                vmem_limit_bytes=VMEM_LIMIT)


def _sigmoid(x):
    return 0.5 * jnp.tanh(0.5 * x) + 0.5


def _silu(x):
    return x * _sigmoid(x)


def _expm1(z):
    t = jnp.tanh(0.5 * z)
    return 2.0 * t / (1.0 - t)


def _shift_rows(x, d, fill=0.0):
    rows = lax.broadcasted_iota(jnp.int32, x.shape, 0)
    return jnp.where(rows >= d, pltpu.roll(x, d, axis=0), fill)


def _scan_steps(a, b, axis, need_a):
    n = a.shape[axis]
    idx = lax.broadcasted_iota(jnp.int32, a.shape, axis)
    d = 1
    while d < n:
        keep = idx >= d
        b = b + a * jnp.where(keep, pltpu.roll(b, d, axis=axis), 0.0)
        if need_a or 2 * d < n:
            a = a * jnp.where(keep, pltpu.roll(a, d, axis=axis), 1.0)
        d *= 2
    return a, b


def _linear_scan_rows(a, b, a_ref, b_ref, c_ref):
    s = a.shape[0]
    n_sub = F32_SUBLANES
    g = s // n_sub
    sub = lax.broadcasted_iota(jnp.int32, a.shape, 0) % n_sub
    d = 1
    while d < n_sub:
        keep = sub >= d
        b = b + a * jnp.where(keep, pltpu.roll(b, d, axis=0), 0.0)
        a = a * jnp.where(keep, pltpu.roll(a, d, axis=0), 1.0)
        d *= 2
    a_ref[...] = a
    b_ref[...] = b
    last = pl.ds(n_sub - 1, g, stride=n_sub)
    _, tot = _scan_steps(a_ref[last, :], b_ref[last, :], 0, False)
    carry = _shift_rows(tot, 1)
    for j in range(n_sub):
        c_ref[pl.ds(j, g, stride=n_sub), :] = carry
    return b + a * c_ref[...]


def _mm(x, ws, epi, outs, *, tm, tn, name, rows=(), cols=()):
    M, K = x.shape
    n_j = outs[0][0] // tn
    n_i = M // tm
    assert M % tm == 0 and all(o[0] == n_j * tn for o in outs)
    nw = len(ws)

    def w_block(i, j):
        return jnp.where(i == 0, j, n_j - 1)

    in_specs = [pl.BlockSpec((tm, K), lambda i, j: (i, 0))]
    args = [x]
    w_mode = dict(pipeline_mode=pl.Buffered(1)) if n_j == 1 else {}
    for w, lead, off in ws:
        nl = len(lead)
        in_specs.append(pl.BlockSpec(
            (None,) * nl + (K, tn),
            lambda i, j, lead=lead, off=off: tuple(lead) + (0, off + w_block(i, j)), **w_mode))
        args.append(w)
    for a, width, col_fn in rows:
        in_specs.append(pl.BlockSpec((tm, width), lambda i, j, col_fn=col_fn: (i, col_fn(j))))
        args.append(a)
    for a, off in cols:
        in_specs.append(pl.BlockSpec((a.shape[0], tn), lambda i, j, off=off: (0, off + j)))
        args.append(a)

    def kern(*refs):
        x_ref = refs[0]
        w_refs = refs[1:1 + nw]
        row_refs = refs[1 + nw:1 + nw + len(rows)]
        col_refs = refs[1 + nw + len(rows):1 + nw + len(rows) + len(cols)]
        out_refs = refs[1 + nw + len(rows) + len(cols):-nw]
        wb_refs = refs[-nw:]
        j = pl.program_id(1)

        @pl.when(pl.program_id(0) == 0)
        def _():
            for w_ref, wb in zip(w_refs, wb_refs):
                wb[j] = w_ref[...].astype(BF16)

        xv = x_ref[...].astype(BF16)
        accs = [jnp.dot(xv, wb[j], preferred_element_type=F32) for wb in wb_refs]
        res = epi(accs, [r[...] for r in row_refs], [c[...] for c in col_refs], j)
        for o, r in zip(out_refs, res):
            o[...] = r.astype(o.dtype)

    out = pl.pallas_call(
        kern,
        grid=(n_i, n_j),
        in_specs=in_specs,
        out_specs=[pl.BlockSpec((tm, tn), lambda i, j: (i, j)) for _ in outs],
        out_shape=[jax.ShapeDtypeStruct((M, n), dt) for n, dt in outs],
        scratch_shapes=[pltpu.VMEM((n_j, K, tn), BF16) for _ in ws],
        compiler_params=_cparams(2),
        name=name,
    )(*args)
    return out


def _epi_plain(accs, rows, cols, j):
    return (accs[0],)


def _epi_swiglu(accs, rows, cols, j):
    return (_silu(accs[0]) * accs[1],)


def _epi_ln(accs, rows, cols, j):
    z = ALPHA * rows[0] + accs[0]
    mu = jnp.mean(z, axis=-1, keepdims=True)
    d = z - mu
    var = jnp.mean(d * d, axis=-1, keepdims=True)
    y = d * lax.rsqrt(var + LN_EPS) * cols[0] + cols[1]
    return y, y


def _mm_ln(h, w, lead, xf, g, b, *, name, tm=1024):
    D = xf.shape[1]
    return _mm(h, [(w, lead, 0)], _epi_ln, [(D, F32), (D, BF16)], tm=tm, tn=D, name=name,
               rows=[(xf, D, lambda j: 0)], cols=[(g.reshape(1, D), 0), (b.reshape(1, D), 0)])


def _dense_ffn(xf, xb, w_gu, w_down, layer, g, b):
    f = w_down.shape[1]
    tn = 256
    (h,) = _mm(xb, [(w_gu, (layer,), 0), (w_gu, (layer,), f // tn)], _epi_swiglu, [(f, BF16)],
               tm=2048, tn=tn, name="ffn_up")
    return _mm_ln(h, w_down, (layer,), xf, g, b, name="ffn_down_ln")


def _short_conv_mixer(xf, xb, seq, w_in, conv_w, w_out, g, b):
    D = xf.shape[1]
    tn = 256
    nb = D // tn

    def epi(accs, rows, cols, j):
        b_gate, c_gate, v = accs
        cw = cols[0]
        cv = c_gate * v
        y = cw[2:3] * cv + cw[1:2] * _shift_rows(cv, 1) + cw[0:1] * _shift_rows(cv, 2)
        return (b_gate * y,)

    (y,) = _mm(xb, [(w_in, (), 0), (w_in, (), nb), (w_in, (), 2 * nb)], epi, [(D, BF16)],
               tm=seq, tn=tn, name="conv_in", cols=[(conv_w, 0)])
    return _mm_ln(y, w_out, (), xf, g, b, name="conv_out_ln")


def _rwkv_in(xf, seq, mix, w_r, w_k, w_v, w0, w1, w2, a0, a1, a2, g1, g2, *, tm=512):
    T, D = xf.shape
    ws = (w_r, w_k, w_v, w1, w2, a1, a2, g1, g2)
    tiles_per_seq = seq // tm
    halo = F32_SUBLANES

    def kern(x_ref, prev_ref, mix_ref, w0_ref, a0_ref, *refs):
        w_refs = refs[:len(ws)]
        r_ref, k_ref, v_ref, ld_ref, a_ref, g_ref = refs[len(ws):len(ws) + 6]
        wb = refs[len(ws) + 6:]
        i = pl.program_id(0)

        @pl.when(i == 0)
        def _():
            for w_ref, b_ref in zip(w_refs, wb):
                b_ref[...] = w_ref[...].astype(BF16)

        x = x_ref[...]
        above = jnp.where(i % tiles_per_seq == 0, 0.0, prev_ref[halo - 1:halo, :])
        rows = lax.broadcasted_iota(jnp.int32, x.shape, 0)
        xx = jnp.where(rows == 0, above, pltpu.roll(x, 1, axis=0)) - x
        m = mix_ref[...]
        xr, xw, xk, xv, xa, xg = ((x + xx * m[j:j + 1]).astype(BF16) for j in range(6))

        def dot(u, w):
            return jnp.dot(u.astype(BF16), w[...], preferred_element_type=F32)

        r_ref[...] = dot(xr, wb[0]).astype(r_ref.dtype)
        k_ref[...] = dot(xk, wb[1]).astype(k_ref.dtype)
        v_ref[...] = dot(xv, wb[2]).astype(v_ref.dtype)
        zw = w0_ref[...] + dot(jnp.tanh(dot(xw, wb[3])), wb[4])
        w_log = -(jnp.maximum(-zw, 0.0) + jnp.log(1.0 + jnp.exp(-jnp.abs(zw)))) - 0.5
        ld_ref[...] = -jnp.exp(w_log)
        a_ref[...] = _sigmoid(a0_ref[...] + dot(dot(xa, wb[5]), wb[6]))
        g_ref[...] = dot(_sigmoid(dot(xg, wb[7])), wb[8]).astype(g_ref.dtype)

    row = pl.BlockSpec((tm, D), lambda i: (i, 0))
    full = lambda a: pl.BlockSpec(a.shape, lambda i: (0, 0), pipeline_mode=pl.Buffered(1))
    vec = pl.BlockSpec((1, D), lambda i: (0, 0))
    return pl.pallas_call(
        kern,
        grid=(T // tm,),
        in_specs=[row,
                  pl.BlockSpec((halo, D), lambda i: (jnp.maximum(i * (tm // halo) - 1, 0), 0)),
                  pl.BlockSpec(mix.shape, lambda i: (0, 0)), vec, vec] + [full(w) for w in ws],
        out_specs=[row] * 6,
        out_shape=[jax.ShapeDtypeStruct((T, D), dt) for dt in (BF16, BF16, BF16, F32, F32, BF16)],
        scratch_shapes=[pltpu.VMEM(w.shape, BF16) for w in ws],
        compiler_params=_cparams(1),
        name="rwkv_in",
    )(xf, xf, mix, w0.reshape(1, D), a0.reshape(1, D), *ws)


def _rwkv_scan(r, k, v, ld, a, g, seq, k_k, k_a, r_k, gn_g, gn_b):
    T, D = r.shape
    C = RWKV_CHUNK
    N = RWKV_HEAD
    PW = 2 * N
    NP = D // PW
    NB = RWKV_SEQS_PER_STEP
    n_c = seq // C

    def kern(r_ref, k_ref, v_ref, ld_ref, a_ref, g_ref, kk_ref, ka_ref, rk_ref, gg_ref, gb_ref,
             o_ref, st_ref):
        @pl.when(pl.program_id(1) == 0)
        def _():
            st_ref[...] = jnp.zeros_like(st_ref)

        lane = lax.broadcasted_iota(jnp.int32, (C, PW), 1)
        rowi = lax.broadcasted_iota(jnp.int32, (C, PW), 0)
        lo = lane < N
        col = jnp.where(lo, lane, lane - N)
        strict = rowi > col
        incl = rowi >= col
        lo2 = lax.broadcasted_iota(jnp.int32, (2 * C, PW), 1) < N
        kr = lax.broadcasted_iota(jnp.int32, (PW, PW), 0) < N
        kc = lax.broadcasted_iota(jnp.int32, (PW, PW), 1) < N
        same_head = kr == kc
        ones_bd = same_head.astype(BF16)
        tri = (lax.broadcasted_iota(jnp.int32, (C, C), 0)
               >= lax.broadcasted_iota(jnp.int32, (C, C), 1)).astype(BF16)

        def pairs(x):
            return [x[:, p * PW:(p + 1) * PW] for p in range(NP)]

        def head_sum(x):
            xs = jnp.concatenate(pairs(x), axis=0).astype(BF16)
            s = jnp.dot(xs, ones_bd, preferred_element_type=F32)
            return jnp.concatenate([s[p * C:(p + 1) * C] for p in range(NP)], axis=1)

        at, bt, kt, rt, vp, gl, bonus = [], [], [], [], [], [], []
        for bb in range(NB):
            r_all, k_all, v_all = (t[bb].astype(F32) for t in (r_ref, k_ref, v_ref))
            a_all = a_ref[bb]
            ld_all = ld_ref[bb]
            ld_hi = ld_all.astype(BF16)
            ld_rest = ld_all - ld_hi.astype(F32)
            ld_mid = ld_rest.astype(BF16)
            ld_lo = (ld_rest - ld_mid.astype(F32)).astype(BF16)
            cum = (jnp.dot(tri, ld_hi, preferred_element_type=F32)
                   + jnp.dot(tri, ld_mid, preferred_element_type=F32)
                   + jnp.dot(tri, ld_lo, preferred_element_type=F32))
            kk = k_all * kk_ref[...]
            kk = kk * lax.rsqrt(jnp.maximum(head_sum(kk * kk), 1e-24))
            km = k_all * (1.0 + (a_all - 1.0) * ka_ref[...])
            gam = jnp.exp(cum)
            gam_inv = jnp.exp(-cum)
            at += pairs(-kk * jnp.exp(cum - ld_all))
            bt += pairs(kk * a_all * gam_inv)
            kt += pairs(km * gam_inv)
            rt += pairs(r_all * gam)
            vp += pairs(v_all)
            gl += pairs(gam[C - 1:C, :])
            bonus.append(head_sum(r_all * km * rk_ref[...]) * v_all)

        def nt(x, y):
            return lax.dot_general(x, y, (((1,), (1,)), ((), ())), preferred_element_type=F32)

        def split(x):
            z = jnp.zeros_like(x)
            return jnp.concatenate([jnp.where(lo, x, z), jnp.where(lo, z, x)], axis=0).astype(BF16)

        chains = range(NB * NP)
        ar = [jnp.concatenate([at[p], rt[p]], axis=0) for p in chains]
        ar_b = [x.astype(BF16) for x in ar]
        bk = [jnp.concatenate([bt[p], kt[p]], axis=0) for p in chains]
        kb_b = [jnp.concatenate([kt[p], bt[p]], axis=0).astype(BF16) for p in chains]
        st = [st_ref[p] for p in chains]
        m0 = [nt(jnp.where(lo2, ar[p], 0.0).astype(BF16), bk[p].astype(BF16)) for p in chains]
        m1 = [nt(jnp.where(lo2, 0.0, ar[p]).astype(BF16), kb_b[p]) for p in chains]
        ars = [nt(ar_b[p], st[p].astype(BF16)) for p in chains]
        lmat = [jnp.where(strict, jnp.where(lo, m0[p][:C], m1[p][:C]), 0.0).astype(BF16)
                for p in chains]
        akm = [jnp.where(strict, jnp.where(lo, m1[p][:C], m0[p][:C]), 0.0).astype(BF16)
               for p in chains]
        v_hi_lo = [jnp.concatenate([jnp.where(lo, 0.0, vp[p]), jnp.where(lo, vp[p], 0.0)],
                                   axis=0).astype(BF16) for p in chains]
        u = [ars[p][:C] + jnp.dot(akm[p], v_hi_lo[p], preferred_element_type=F32)
             for p in chains]
        n = 1
        while n < C:
            u = [u[p] + jnp.dot(lmat[p], split(u[p]), preferred_element_type=F32)
                 for p in chains]
            n *= 2
            if n < C:
                lmat = [jnp.dot(lmat[p], split(lmat[p]),
                                preferred_element_type=F32).astype(BF16) for p in chains]
        ys = []
        for p in chains:
            rmat = jnp.concatenate([jnp.where(incl, m0[p][C:], 0.0),
                                    jnp.where(incl, m1[p][C:], 0.0)], axis=1).astype(BF16)
            u0, u1 = jnp.where(lo, u[p], 0.0), jnp.where(lo, 0.0, u[p])
            v0, v1 = jnp.where(lo, vp[p], 0.0), jnp.where(lo, 0.0, vp[p])
            uvs = jnp.concatenate([u0, v0, v1, u1], axis=0).astype(BF16)
            ys.append(ars[p][C:] + jnp.dot(rmat, uvs, preferred_element_type=F32))
            uv = jnp.concatenate([u[p], vp[p]], axis=0).astype(BF16)
            upd = lax.dot_general(uv, (bk[p] * gl[p]).astype(BF16), (((0,), (0,)), ((), ())),
                                  preferred_element_type=F32)
            st_ref[p] = st[p] * gl[p] + jnp.where(same_head, upd, 0.0)

        for bb in range(NB):
            y = jnp.concatenate(ys[bb * NP:(bb + 1) * NP], axis=1)
            dy = y - head_sum(y) * (1.0 / N)
            var = head_sum(dy * dy) * (1.0 / N)
            yn = dy * lax.rsqrt(var + RWKV_GN_EPS) * gg_ref[...] + gb_ref[...]
            o_ref[bb] = ((yn + bonus[bb]) * g_ref[bb].astype(F32)).astype(o_ref.dtype)

    n_seq = T // seq
    row = pl.BlockSpec((NB, C, D), lambda bi, c: (bi, c, 0))
    vec = pl.BlockSpec((1, D), lambda bi, c: (0, 0))
    seqs = [t.reshape(n_seq, seq, D) for t in (r, k, v, ld, a, g)]
    out = pl.pallas_call(
        kern,
        grid=(n_seq // NB, n_c),
        in_specs=[row] * 6 + [vec] * 5,
        out_specs=row,
        out_shape=jax.ShapeDtypeStruct((n_seq, seq, D), BF16),
        scratch_shapes=[pltpu.VMEM((NB * NP, PW, PW), F32)],
        compiler_params=_cparams(2),
        name="rwkv_scan",
    )(*seqs, k_k.reshape(1, D), k_a.reshape(1, D), r_k.reshape(1, D),
      gn_g.reshape(1, D), gn_b.reshape(1, D))
    return out.reshape(T, D)


def _rwkv7_mixer(xf, seq, mix, w_r, w_k, w_v, w0, w1, w2, a0, a1, a2, g1, g2,
                 k_k, k_a, r_k, gn_g, gn_b, w_o, g, b):
    r, k, v, ld, a, gate = _rwkv_in(xf, seq, mix, w_r, w_k, w_v, w0, w1, w2, a0, a1, a2, g1, g2)
    y = _rwkv_scan(r, k, v, ld, a, gate, seq, k_k, k_a, r_k, gn_g, gn_b)
    return _mm_ln(y, w_o, (), xf, g, b, name="rwkv_out_ln")


def _rglru_core(proj, pos3, seq, conv_w, conv_b, w_ga, b_ga, w_gx, b_gx, lam):
    T = proj.shape[0]
    W = LRU_BW
    n_conv = conv_w.shape[0]

    def kern(gate_ref, u_ref, pos_ref, cw_ref, cb_ref, wga_ref, bga_ref, wgx_ref, bgx_ref,
             lam_ref, o_ref, sa_ref, sb_ref, sc_ref, pad_ref):
        u = u_ref[...]
        cw = cw_ref[...]
        uc = cw[n_conv - 1:n_conv] * u + cb_ref[...]
        pad_ref[0:F32_SUBLANES, :] = jnp.zeros((F32_SUBLANES, W), F32)
        pad_ref[F32_SUBLANES:, :] = u
        for d in range(1, n_conv):
            uc = uc + cw[n_conv - 1 - d:n_conv - d] * pad_ref[F32_SUBLANES - d:F32_SUBLANES - d + seq, :]
        ub = uc.astype(BF16)
        rg = _sigmoid(jnp.dot(ub, wga_ref[...].astype(BF16), preferred_element_type=F32)
                      + bga_ref[...])
        ig = _sigmoid(jnp.dot(ub, wgx_ref[...].astype(BF16), preferred_element_type=F32)
                      + bgx_ref[...])
        nl = -lam_ref[...]
        softplus = jnp.maximum(nl, 0.0) + jnp.log1p(jnp.exp(-jnp.abs(nl)))
        log_a = -LRU_C * rg * softplus
        reset = pos_ref[...] == 0
        a = jnp.where(reset, 0.0, jnp.exp(log_a))
        mult = jnp.where(reset, 1.0, jnp.sqrt(-_expm1(2.0 * log_a)))
        hb = _linear_scan_rows(a, mult * (ig * uc), sa_ref, sb_ref, sc_ref)
        gt = gate_ref[...]
        gelu = 0.5 * gt * (1.0 + jnp.tanh(math.sqrt(2.0 / math.pi) * (gt + 0.044715 * gt * gt * gt)))
        o_ref[...] = (gelu * hb).astype(o_ref.dtype)

    vec = pl.BlockSpec((1, W), lambda bi, j: (0, j))
    blk = pl.BlockSpec((None, W, W), lambda bi, j: (j, 0, 0))
    return pl.pallas_call(
        kern,
        grid=(T // seq, LRU_BLOCKS),
        in_specs=[pl.BlockSpec((seq, W), lambda bi, j: (bi, j)),
                  pl.BlockSpec((seq, W), lambda bi, j: (bi, LRU_BLOCKS + j)),
                  pl.BlockSpec((None, seq, 1), lambda bi, j: (bi, 0, 0)),
                  pl.BlockSpec((n_conv, W), lambda bi, j: (0, j)),
                  vec, blk, vec, blk, vec, vec],
        out_specs=pl.BlockSpec((seq, W), lambda bi, j: (bi, j)),
        out_shape=jax.ShapeDtypeStruct((T, D_RNN), BF16),
        scratch_shapes=[pltpu.VMEM((seq, W), F32)] * 3 + [pltpu.VMEM((seq + F32_SUBLANES, W), F32)],
        compiler_params=_cparams(2),
        name="rglru_core",
    )(proj, proj, pos3, conv_w, conv_b.reshape(1, -1), w_ga, b_ga.reshape(1, -1),
      w_gx, b_gx.reshape(1, -1), lam.reshape(1, -1))


def _rglru_mixer(xf, xb, pos3, seq, w_in, conv_w, conv_b, w_ga, b_ga, w_gx, b_gx, lam, w_out, g, b):
    (proj,) = _mm(xb, [(w_in, (), 0)], _epi_plain, [(2 * D_RNN, F32)], tm=2048, tn=512,
                  name="rglru_in")
    y = _rglru_core(proj, pos3, seq, conv_w, conv_b, w_ga, b_ga, w_gx, b_gx, lam)
    return _mm_ln(y, w_out, (), xf, g, b, name="rglru_out_ln")


def _rope_tables(pos_col, freq, *, tm=2048):
    T = pos_col.shape[0]
    half = freq.shape[1]

    def kern(pos_ref, f_ref, cos_ref, sin_ref):
        ang = pos_ref[...].astype(F32) * f_ref[...]
        cos_ref[...] = jnp.cos(ang)
        sin_ref[...] = jnp.sin(ang)

    out = pl.BlockSpec((tm, half), lambda i: (i, 0))
    return pl.pallas_call(
        kern,
        grid=(T // tm,),
        in_specs=[pl.BlockSpec((tm, 1), lambda i: (i, 0)), pl.BlockSpec((1, half), lambda i: (0, 0))],
        out_specs=[out, out],
        out_shape=[jax.ShapeDtypeStruct((T, half), F32)] * 2,
        compiler_params=_cparams(1),
        name="rope_tables",
    )(pos_col, freq)


def _retention_core(qk, vg, seq):
    T = qk.shape[0]
    L, CH = RET_SUPER, RET_CHUNK
    H, DK, DV = RET_HEADS, RET_DK, RET_DV
    n_l = seq // L

    def kern(qk_ref, vg_ref, o_ref, st_ref, decay_ref, qdec_ref, kdec_ref):
        @pl.when(jnp.logical_and(pl.program_id(0) == 0, pl.program_id(1) == 0))
        def _():
            ri = lax.broadcasted_iota(jnp.int32, (L, L), 0)
            ci = lax.broadcasted_iota(jnp.int32, (L, L), 1)
            dist = jnp.abs(ri - ci).astype(F32)
            allowed = ci // CH <= ri // CH
            idx = lax.broadcasted_iota(jnp.int32, (L, 1), 0).astype(F32)
            for h in range(H):
                log_g = math.log1p(-(2.0 ** (-5.0 - h)))
                decay_ref[h] = jnp.where(allowed, jnp.exp(dist * log_g), 0.0)
                qdec_ref[h] = jnp.exp((idx + 1.0) * log_g)
                kdec_ref[h] = jnp.exp((L - 1.0 - idx) * log_g)

        @pl.when(pl.program_id(1) == 0)
        def _():
            st_ref[...] = jnp.zeros_like(st_ref)

        for h in range(H):
            q = qk_ref[:, h * DK:(h + 1) * DK]
            k = qk_ref[:, (H + h) * DK:(H + h + 1) * DK]
            v = vg_ref[:, h * DV:(h + 1) * DV]
            g = vg_ref[:, (H + h) * DV:(H + h + 1) * DV]
            st = st_ref[h]
            scores = lax.dot_general(q, k, (((1,), (1,)), ((), ())), preferred_element_type=F32)
            scores = (scores * decay_ref[h]).astype(BF16)
            qd = (q.astype(F32) * qdec_ref[h]).astype(BF16)
            y = (jnp.dot(scores, v, preferred_element_type=F32)
                 + jnp.dot(qd, st.astype(BF16), preferred_element_type=F32))
            kd = (k.astype(F32) * kdec_ref[h]).astype(BF16)
            c_dec = math.exp(L * math.log1p(-(2.0 ** (-5.0 - h))))
            st_ref[h] = st * c_dec + lax.dot_general(kd, v, (((0,), (0,)), ((), ())),
                                                     preferred_element_type=F32)
            yn = y * lax.rsqrt(jnp.mean(y * y, axis=-1, keepdims=True) + RET_GN_EPS)
            o_ref[:, h * DV:(h + 1) * DV] = (_silu(g.astype(F32)) * yn).astype(o_ref.dtype)

    return pl.pallas_call(
        kern,
        grid=(T // seq, n_l),
        in_specs=[pl.BlockSpec((L, 2 * H * DK), lambda bi, l: (bi * n_l + l, 0)),
                  pl.BlockSpec((L, 2 * H * DV), lambda bi, l: (bi * n_l + l, 0))],
        out_specs=pl.BlockSpec((L, H * DV), lambda bi, l: (bi * n_l + l, 0)),
        out_shape=jax.ShapeDtypeStruct((T, H * DV), BF16),
        scratch_shapes=[pltpu.VMEM((H, DK, DV), F32), pltpu.VMEM((H, L, L), F32),
                        pltpu.VMEM((H, L, 1), F32), pltpu.VMEM((H, L, 1), F32)],
        compiler_params=_cparams(2),
        name="retention_core",
    )(qk, vg)


def _retention_mixer(xf, xb, pos_col, seq, w_in, w_o, g, b):
    D = xf.shape[1]
    half = RET_DK // 2
    freq = (ROPE_BASE ** -jnp.linspace(0.0, 1.0, half, dtype=F32)).reshape(1, half)
    cos, sin = _rope_tables(pos_col, freq)
    n_q = D // RET_DK

    def epi_rope(accs, rows, cols, j):
        t1, t2 = accs[0][:, :half], accs[0][:, half:]
        c, s = rows
        scale = jnp.where(j >= n_q, RET_DK ** -0.5, 1.0).astype(F32)
        return (jnp.concatenate([t1 * c - t2 * s, t1 * s + t2 * c], axis=1) * scale,)

    (qk,) = _mm(xb, [(w_in, (), 0)], epi_rope, [(2 * D, BF16)], tm=2048, tn=RET_DK,
                name="ret_qk_rope", rows=[(cos, half, lambda j: 0), (sin, half, lambda j: 0)])
    (vg,) = _mm(xb, [(w_in, (), 2 * D // 512)], _epi_plain, [(4 * D, BF16)], tm=2048, tn=512,
                name="ret_vg")
    y = _retention_core(qk, vg, seq)
    return _mm_ln(y, w_o, (), xf, g, b, name="ret_out_ln")


def _moe_router(xf, w_router, *, tm=1024):
    T, D = xf.shape
    E = w_router.shape[1]
    LANES = 128
    wpad = jnp.pad(w_router, ((0, 0), (0, LANES - E)))

    def kern(x_ref, w_ref, id_ref, p_ref):
        x = x_ref[...]
        w = w_ref[...]
        x_hi = x.astype(BF16)
        x_lo = (x - x_hi.astype(F32)).astype(BF16)
        w_hi = w.astype(BF16)
        w_lo = (w - w_hi.astype(F32)).astype(BF16)
        logits = (jnp.dot(x_hi, w_hi, preferred_element_type=F32)
                  + jnp.dot(x_lo, w_hi, preferred_element_type=F32)
                  + jnp.dot(x_hi, w_lo, preferred_element_type=F32))
        lane = lax.broadcasted_iota(jnp.int32, logits.shape, 1).astype(F32)
        neg = jnp.float32(-jnp.inf)
        l1 = jnp.where(lane < E, logits, neg)
        m1 = jnp.max(l1, axis=-1, keepdims=True)
        i1 = jnp.min(jnp.where(l1 == m1, lane, float(LANES)), axis=-1, keepdims=True)
        l2 = jnp.where(lane == i1, neg, l1)
        m2 = jnp.max(l2, axis=-1, keepdims=True)
        i2 = jnp.min(jnp.where(l2 == m2, lane, float(LANES)), axis=-1, keepdims=True)
        e = jnp.exp(m2 - m1)
        p1 = 1.0 / (1.0 + e)
        p2 = e / (1.0 + e)
        id_ref[...] = jnp.where(lane == 0, i1, jnp.where(lane == 1, i2, 0.0)).astype(jnp.int32)
        p_ref[...] = jnp.where(lane == 0, p1, jnp.where(lane == 1, p2, 0.0))

    out = pl.BlockSpec((tm, LANES), lambda i: (i, 0))
    return pl.pallas_call(
        kern,
        grid=(T // tm,),
        in_specs=[pl.BlockSpec((tm, D), lambda i: (i, 0)), pl.BlockSpec((D, LANES), lambda i: (0, 0))],
        out_specs=[out, out],
        out_shape=[jax.ShapeDtypeStruct((T, LANES), jnp.int32), jax.ShapeDtypeStruct((T, LANES), F32)],
        compiler_params=_cparams(1),
        name="moe_router",
    )(xf, wpad)


def _moe_plan(ids, n_tiles):
    T = ids.shape[0]
    e_flat = ids.reshape(-1)
    onehot = (e_flat[:, None] == jnp.arange(N_EXPERTS, dtype=jnp.int32)[None, :]).astype(jnp.int32)
    csum = jnp.cumsum(onehot, axis=0)
    rank = jnp.sum((csum - onehot) * onehot, axis=1)
    counts = csum[-1]
    padded = ((counts + MOE_TILE - 1) // MOE_TILE) * MOE_TILE
    ends = jnp.cumsum(padded)
    starts = ends - padded
    pos = (jnp.sum(onehot * starts[None, :], axis=1) + rank).astype(jnp.int32)
    tile_start = jnp.arange(n_tiles, dtype=jnp.int32) * MOE_TILE
    tile_e = jnp.sum((tile_start[:, None] >= ends[None, :]).astype(jnp.int32), axis=1)
    tile_e = jnp.minimum(tile_e, N_EXPERTS - 1).astype(jnp.int32)
    n_used = (ends[-1] // MOE_TILE).astype(jnp.int32).reshape(1)
    last_tile = jnp.where(padded > 0, ends // MOE_TILE - 1, -1)
    tail = n_tiles - N_EXPERTS + jnp.arange(N_EXPERTS, dtype=jnp.int32)
    tail = jnp.where(tail >= n_used[0], tail, -1)
    fill_tiles = jnp.concatenate([last_tile, tail]).astype(jnp.int32)
    return pos.reshape(T, TOP_K), tile_e, n_used, fill_tiles


def _moe_dispatch(xf, pos, fill_tiles, n_rows, *, tq=1024):
    T, D = xf.shape
    n_steps = T // tq
    pos_blocks = pos.reshape(n_steps, 1, tq * TOP_K)
    n_fill = fill_tiles.shape[0]

    def kern(fill_ref, pos_ref, x_ref, xs_ref, zero_ref, sem, zsem):
        @pl.when(pl.program_id(0) == 0)
        def _():
            zero_ref[...] = jnp.zeros_like(zero_ref)

            def fill(z):
                row0 = pl.multiple_of(fill_ref[z] * MOE_TILE, MOE_TILE)
                return pltpu.make_async_copy(zero_ref, xs_ref.at[pl.ds(row0, MOE_TILE)], zsem)

            for z in range(n_fill):
                @pl.when(fill_ref[z] >= 0)
                def _():
                    fill(z).start()
            for z in range(n_fill):
                @pl.when(fill_ref[z] >= 0)
                def _():
                    fill(z).wait()

        def copy(r, k):
            dst = pos_ref[0, TOP_K * r + k]
            return pltpu.make_async_copy(x_ref.at[pl.ds(r, 1)], xs_ref.at[pl.ds(dst, 1)], sem)

        def start(r, c):
            for k in range(TOP_K):
                copy(r, k).start(priority=k % 2)
            return c

        def wait(r, c):
            for k in range(TOP_K):
                copy(r, k).wait()
            return c

        lax.fori_loop(0, tq, start, 0, unroll=DMA_UNROLL)
        lax.fori_loop(0, tq, wait, 0, unroll=DMA_UNROLL)

    return pl.pallas_call(
        kern,
        grid_spec=pltpu.PrefetchScalarGridSpec(
            num_scalar_prefetch=1,
            grid=(n_steps,),
            in_specs=[pl.BlockSpec((None, 1, tq * TOP_K), lambda i, ft: (i, 0, 0),
                                   memory_space=pltpu.SMEM),
                      pl.BlockSpec((tq, D), lambda i, ft: (i, 0))],
            out_specs=pl.BlockSpec(memory_space=pl.ANY),
            scratch_shapes=[pltpu.VMEM((MOE_TILE, D), F32), pltpu.SemaphoreType.DMA,
                            pltpu.SemaphoreType.DMA]),
        out_shape=jax.ShapeDtypeStruct((n_rows, D), F32),
        compiler_params=_cparams(1),
        name="moe_dispatch",
    )(fill_tiles, pos_blocks, xf)


def _moe_experts(xs, tile_e, n_used, w_gu, w_down, layer):
    P, D = xs.shape
    F = w_down.shape[2]
    tm = MOE_TILE
    n_tiles = P // tm
    tn_up = 1792
    n_f = F // tn_up
    tn_dn = 1024

    def tile(i, nu):
        return jnp.minimum(i, nu[0] - 1)

    def fresh(te, nu):
        i = pl.program_id(1)
        prev = te[jnp.maximum(i - 1, 0)]
        return jnp.logical_and(i < nu[0], jnp.logical_or(i == 0, te[i] != prev))

    def up_kern(te, nu, x_ref, wg_ref, wu_ref, h_ref, wgb, wub):
        @pl.when(fresh(te, nu))
        def _():
            wgb[...] = wg_ref[...].astype(BF16)
            wub[...] = wu_ref[...].astype(BF16)

        @pl.when(pl.program_id(1) < nu[0])
        def _():
            xv = x_ref[...].astype(BF16)
            gt = jnp.dot(xv, wgb[...], preferred_element_type=F32)
            up = jnp.dot(xv, wub[...], preferred_element_type=F32)
            h_ref[...] = (_silu(gt) * up).astype(h_ref.dtype)

        @pl.when(pl.program_id(1) >= nu[0])
        def _():
            h_ref[...] = jnp.zeros_like(h_ref)

    h = pl.pallas_call(
        up_kern,
        grid_spec=pltpu.PrefetchScalarGridSpec(
            num_scalar_prefetch=2,
            grid=(n_f, n_tiles),
            in_specs=[pl.BlockSpec((tm, D), lambda f, i, te, nu: (tile(i, nu), 0)),
                      pl.BlockSpec((None, None, D, tn_up),
                                   lambda f, i, te, nu: (layer, te[tile(i, nu)], 0, f)),
                      pl.BlockSpec((None, None, D, tn_up),
                                   lambda f, i, te, nu: (layer, te[tile(i, nu)], 0, n_f + f))],
            out_specs=pl.BlockSpec((tm, tn_up), lambda f, i, te, nu: (i, f)),
            scratch_shapes=[pltpu.VMEM((D, tn_up), BF16), pltpu.VMEM((D, tn_up), BF16)]),
        out_shape=jax.ShapeDtypeStruct((P, F), BF16),
        compiler_params=_cparams(2),
        name="moe_up",
    )(tile_e, n_used, xs, w_gu, w_gu)

    def down_kern(te, nu, h_ref, wd_ref, y_ref, wdb):
        @pl.when(fresh(te, nu))
        def _():
            wdb[...] = wd_ref[...].astype(BF16)

        @pl.when(pl.program_id(1) < nu[0])
        def _():
            y_ref[...] = jnp.dot(h_ref[...], wdb[...], preferred_element_type=F32)

        @pl.when(pl.program_id(1) >= nu[0])
        def _():
            y_ref[...] = jnp.zeros_like(y_ref)

    return pl.pallas_call(
        down_kern,
        grid_spec=pltpu.PrefetchScalarGridSpec(
            num_scalar_prefetch=2,
            grid=(D // tn_dn, n_tiles),
            in_specs=[pl.BlockSpec((tm, F), lambda n, i, te, nu: (tile(i, nu), 0)),
                      pl.BlockSpec((None, None, F, tn_dn),
                                   lambda n, i, te, nu: (layer, te[tile(i, nu)], 0, n))],
            out_specs=pl.BlockSpec((tm, tn_dn), lambda n, i, te, nu: (i, n)),
            scratch_shapes=[pltpu.VMEM((F, tn_dn), BF16)]),
        out_shape=jax.ShapeDtypeStruct((P, D), F32),
        compiler_params=_cparams(2),
        name="moe_down",
    )(tile_e, n_used, h, w_down)


def _moe_combine_ln(xf, ys, pos, probs, g, b, *, tq=1024):
    T, D = xf.shape
    n_steps = T // tq
    pos_blocks = pos.reshape(n_steps, 1, tq * TOP_K)

    def kern(pos_ref, x_ref, p_ref, g_ref, b_ref, ys_ref, of_ref, ob_ref, buf, sem):
        def copy(r, k):
            src = pos_ref[0, TOP_K * r + k]
            return pltpu.make_async_copy(ys_ref.at[pl.ds(src, 1)], buf.at[k, pl.ds(r, 1)], sem)

        def start(r, c):
            for k in range(TOP_K):
                copy(r, k).start(priority=k % 2)
            return c

        def wait(r, c):
            for k in range(TOP_K):
                copy(r, k).wait()
            return c

        lax.fori_loop(0, tq, start, 0, unroll=DMA_UNROLL)
        lax.fori_loop(0, tq, wait, 0, unroll=DMA_UNROLL)
        p = p_ref[...]
        z = ALPHA * x_ref[...] + p[:, 0:1] * buf[0] + p[:, 1:2] * buf[1]
        mu = jnp.mean(z, axis=-1, keepdims=True)
        d = z - mu
        var = jnp.mean(d * d, axis=-1, keepdims=True)
        y = d * lax.rsqrt(var + LN_EPS) * g_ref[...] + b_ref[...]
        of_ref[...] = y
        ob_ref[...] = y.astype(ob_ref.dtype)

    row = pl.BlockSpec((tq, D), lambda i: (i, 0))
    vec = pl.BlockSpec((1, D), lambda i: (0, 0))
    return pl.pallas_call(
        kern,
        grid=(n_steps,),
        in_specs=[pl.BlockSpec((None, 1, tq * TOP_K), lambda i: (i, 0, 0), memory_space=pltpu.SMEM),
                  row, pl.BlockSpec((tq, probs.shape[1]), lambda i: (i, 0)), vec, vec,
                  pl.BlockSpec(memory_space=pl.ANY)],
        out_specs=[row, row],
        out_shape=[jax.ShapeDtypeStruct((T, D), F32), jax.ShapeDtypeStruct((T, D), BF16)],
        scratch_shapes=[pltpu.VMEM((TOP_K, tq, D), F32), pltpu.SemaphoreType.DMA],
        compiler_params=_cparams(1),
        name="moe_combine_ln",
    )(pos_blocks, xf, probs, g.reshape(1, D), b.reshape(1, D), ys)


def _moe_ffn(xf, w_router, w_gu, w_down, layer, g, b):
    T = xf.shape[0]
    n_tiles = T * TOP_K // MOE_TILE + N_EXPERTS
    ids, probs = _moe_router(xf, w_router[layer])
    pos, tile_e, n_used, fill_tiles = _moe_plan(ids[:, :TOP_K], n_tiles)
    xs = _moe_dispatch(xf, pos, fill_tiles, n_tiles * MOE_TILE)
    ys = _moe_experts(xs, tile_e, n_used, w_gu, w_down, layer)
    return _moe_combine_ln(xf, ys, pos, probs, g, b)


def kernel(x, positions, ln_mix_g, ln_mix_b, ln_ffn_g, ln_ffn_b, a_w_in, a_conv_w, a_w_out, b_mix, b_w_r, b_w_k, b_w_v, b_w0, b_w1, b_w2, b_a0, b_a1, b_a2, b_g1, b_g2, b_k_k, b_k_a, b_r_k, b_gn_g, b_gn_b, b_w_o, c_w_in, c_conv_w, c_conv_b, c_w_ga, c_b_ga, c_w_gx, c_b_gx, c_lam, c_w_out, d_w_in, d_w_o, ffn_w_gu, ffn_w_down, moe_w_router, moe_w_gu, moe_w_down):
    bsz, seq, D = x.shape
    T = bsz * seq
    xf = x.reshape(T, D)
    pos3 = positions.reshape(bsz, seq, 1)
    pos_col = positions.reshape(T, 1)

    xf, xb = _short_conv_mixer(xf, xf, seq, a_w_in, a_conv_w, a_w_out, ln_mix_g[0], ln_mix_b[0])
    xf, xb = _dense_ffn(xf, xb, ffn_w_gu, ffn_w_down, 0, ln_ffn_g[0], ln_ffn_b[0])

    xf, xb = _rwkv7_mixer(xf, seq, b_mix, b_w_r, b_w_k, b_w_v, b_w0, b_w1, b_w2, b_a0, b_a1, b_a2,
                          b_g1, b_g2, b_k_k, b_k_a, b_r_k, b_gn_g, b_gn_b, b_w_o,
                          ln_mix_g[1], ln_mix_b[1])
    xf, xb = _moe_ffn(xf, moe_w_router, moe_w_gu, moe_w_down, 0, ln_ffn_g[1], ln_ffn_b[1])

    xf, xb = _rglru_mixer(xf, xb, pos3, seq, c_w_in, c_conv_w, c_conv_b, c_w_ga, c_b_ga, c_w_gx,
                          c_b_gx, c_lam, c_w_out, ln_mix_g[2], ln_mix_b[2])
    xf, xb = _dense_ffn(xf, xb, ffn_w_gu, ffn_w_down, 1, ln_ffn_g[2], ln_ffn_b[2])

    xf, xb = _retention_mixer(xf, xb, pos_col, seq, d_w_in, d_w_o, ln_mix_g[3], ln_mix_b[3])
    xf, xb = _moe_ffn(xf, moe_w_router, moe_w_gu, moe_w_down, 1, ln_ffn_g[3], ln_ffn_b[3])
    return xf.reshape(bsz, seq, D)
```

```python
import functools
import math

import jax
import jax.numpy as jnp
from jax import lax
from jax.experimental import pallas as pl
from jax.experimental.pallas import tpu as pltpu

F32 = jnp.float32
BF16 = jnp.bfloat16

D_MODEL = 1024
DEPTH = 4
ALPHA = (2.0 * DEPTH) ** 0.25
LN_EPS = 1e-5

RWKV_HEAD = 64
RWKV_HEADS = D_MODEL // RWKV_HEAD
RWKV_GN_EPS = 64e-5
RWKV_CHUNK = 64
RWKV_SEQS_PER_STEP = 8

D_RNN = 1280
LRU_BLOCKS = 10
LRU_BW = D_RNN // LRU_BLOCKS
LRU_C = 8.0

RET_HEADS = 4
RET_DK = D_MODEL // RET_HEADS
RET_DV = 2 * D_MODEL // RET_HEADS
RET_CHUNK = 64
RET_SUPER = 256
ROPE_BASE = 10000.0
RET_GN_EPS = 1e-6

N_EXPERTS = 8
TOP_K = 2
MOE_TILE = 512
DMA_UNROLL = 8

VMEM_LIMIT = 56 * 1024 * 1024
F32_SUBLANES = 8


def _cparams(n_grid):
    return pltpu.CompilerParams(dimension_semantics=("arbitrary",) * n_grid,
                                vmem_limit_bytes=VMEM_LIMIT)


def _sigmoid(x):
    return 0.5 * jnp.tanh(0.5 * x) + 0.5


def _silu(x):
    return x * _sigmoid(x)


def _expm1(z):
    t = jnp.tanh(0.5 * z)
    return 2.0 * t / (1.0 - t)


def _shift_rows(x, d, fill=0.0):
    rows = lax.broadcasted_iota(jnp.int32, x.shape, 0)
    return jnp.where(rows >= d, pltpu.roll(x, d, axis=0), fill)


def _scan_steps(a, b, axis, need_a):
    n = a.shape[axis]
    idx = lax.broadcasted_iota(jnp.int32, a.shape, axis)
    d = 1
    while d < n:
        keep = idx >= d
        b = b + a * jnp.where(keep, pltpu.roll(b, d, axis=axis), 0.0)
        if need_a or 2 * d < n:
            a = a * jnp.where(keep, pltpu.roll(a, d, axis=axis), 1.0)
        d *= 2
    return a, b


def _linear_scan_rows(a, b, a_ref, b_ref, c_ref):
    s, w = a.shape
    n_sub = F32_SUBLANES
    g = s // n_sub
    a, b = _scan_steps(a.reshape(g, n_sub, w), b.reshape(g, n_sub, w), 1, True)
    a, b = a.reshape(s, w), b.reshape(s, w)
    a_ref[...] = a
    b_ref[...] = b
    last = pl.ds(n_sub - 1, g, stride=n_sub)
    _, tot = _scan_steps(a_ref[last, :], b_ref[last, :], 0, False)
    carry = _shift_rows(tot, 1)
    for j in range(n_sub):
        c_ref[pl.ds(j, g, stride=n_sub), :] = carry
    return b + a * c_ref[...]


def _mm(x, ws, epi, outs, *, tm, tn, name, rows=(), cols=()):
    M, K = x.shape
    n_j = outs[0][0] // tn
    n_i = M // tm
    assert M % tm == 0 and all(o[0] == n_j * tn for o in outs)
    nw = len(ws)

    def w_block(i, j):
        return jnp.where(i == 0, j, n_j - 1)

    in_specs = [pl.BlockSpec((tm, K), lambda i, j: (i, 0))]
    args = [x]
    w_mode = dict(pipeline_mode=pl.Buffered(1)) if n_j == 1 else {}
    for w, lead, off in ws:
        nl = len(lead)
        in_specs.append(pl.BlockSpec(
            (None,) * nl + (K, tn),
            lambda i, j, lead=lead, off=off: tuple(lead) + (0, off + w_block(i, j)), **w_mode))
        args.append(w)
    for a, width, col_fn in rows:
        in_specs.append(pl.BlockSpec((tm, width), lambda i, j, col_fn=col_fn: (i, col_fn(j))))
        args.append(a)
    for a, off in cols:
        in_specs.append(pl.BlockSpec((a.shape[0], tn), lambda i, j, off=off: (0, off + j)))
        args.append(a)

    def kern(*refs):
        x_ref = refs[0]
        w_refs = refs[1:1 + nw]
        row_refs = refs[1 + nw:1 + nw + len(rows)]
        col_refs = refs[1 + nw + len(rows):1 + nw + len(rows) + len(cols)]
        out_refs = refs[1 + nw + len(rows) + len(cols):-nw]
        wb_refs = refs[-nw:]
        j = pl.program_id(1)

        @pl.when(pl.program_id(0) == 0)
        def _():
            for w_ref, wb in zip(w_refs, wb_refs):
                wb[j] = w_ref[...].astype(BF16)

        xv = x_ref[...].astype(BF16)
        accs = [jnp.dot(xv, wb[j], preferred_element_type=F32) for wb in wb_refs]
        res = epi(accs, [r[...] for r in row_refs], [c[...] for c in col_refs], j)
        for o, r in zip(out_refs, res):
            o[...] = r.astype(o.dtype)

    out = pl.pallas_call(
        kern,
        grid=(n_i, n_j),
        in_specs=in_specs,
        out_specs=[pl.BlockSpec((tm, tn), lambda i, j: (i, j)) for _ in outs],
        out_shape=[jax.ShapeDtypeStruct((M, n), dt) for n, dt in outs],
        scratch_shapes=[pltpu.VMEM((n_j, K, tn), BF16) for _ in ws],
        compiler_params=_cparams(2),
        name=name,
    )(*args)
    return out


def _epi_plain(accs, rows, cols, j):
    return (accs[0],)


def _epi_swiglu(accs, rows, cols, j):
    return (_silu(accs[0]) * accs[1],)


def _epi_ln(accs, rows, cols, j):
    z = ALPHA * rows[0] + accs[0]
    mu = jnp.mean(z, axis=-1, keepdims=True)
    d = z - mu
    var = jnp.mean(d * d, axis=-1, keepdims=True)
    y = d * lax.rsqrt(var + LN_EPS) * cols[0] + cols[1]
    return y, y


def _mm_ln(h, w, lead, xf, g, b, *, name, tm=1024):
    D = xf.shape[1]
    return _mm(h, [(w, lead, 0)], _epi_ln, [(D, F32), (D, BF16)], tm=tm, tn=D, name=name,
               rows=[(xf, D, lambda j: 0)], cols=[(g.reshape(1, D), 0), (b.reshape(1, D), 0)])


def _dense_ffn(xf, xb, w_gu, w_down, layer, g, b):
    f = w_down.shape[1]
    tn = 256
    (h,) = _mm(xb, [(w_gu, (layer,), 0), (w_gu, (layer,), f // tn)], _epi_swiglu, [(f, BF16)],
               tm=2048, tn=tn, name="ffn_up")
    return _mm_ln(h, w_down, (layer,), xf, g, b, name="ffn_down_ln")


def _short_conv_mixer(xf, xb, seq, w_in, conv_w, w_out, g, b):
    D = xf.shape[1]
    tn = 256
    nb = D // tn

    def epi(accs, rows, cols, j):
        b_gate, c_gate, v = accs
        cw = cols[0]
        cv = c_gate * v
        y = cw[2:3] * cv + cw[1:2] * _shift_rows(cv, 1) + cw[0:1] * _shift_rows(cv, 2)
        return (b_gate * y,)

    (y,) = _mm(xb, [(w_in, (), 0), (w_in, (), nb), (w_in, (), 2 * nb)], epi, [(D, BF16)],
               tm=seq, tn=tn, name="conv_in", cols=[(conv_w, 0)])
    return _mm_ln(y, w_out, (), xf, g, b, name="conv_out_ln")


def _rwkv_in(xf, seq, mix, w_r, w_k, w_v, w0, w1, w2, a0, a1, a2, g1, g2, *, tm=512):
    T, D = xf.shape
    ws = (w_r, w_k, w_v, w1, w2, a1, a2, g1, g2)
    tiles_per_seq = seq // tm
    halo = F32_SUBLANES

    def kern(x_ref, prev_ref, mix_ref, w0_ref, a0_ref, *refs):
        w_refs = refs[:len(ws)]
        r_ref, k_ref, v_ref, ld_ref, a_ref, g_ref = refs[len(ws):len(ws) + 6]
        wb = refs[len(ws) + 6:]
        i = pl.program_id(0)

        @pl.when(i == 0)
        def _():
            for w_ref, b_ref in zip(w_refs, wb):
                b_ref[...] = w_ref[...].astype(BF16)

        x = x_ref[...]
        above = jnp.where(i % tiles_per_seq == 0, 0.0, prev_ref[halo - 1:halo, :])
        rows = lax.broadcasted_iota(jnp.int32, x.shape, 0)
        xx = jnp.where(rows == 0, above, pltpu.roll(x, 1, axis=0)) - x
        m = mix_ref[...]
        xr, xw, xk, xv, xa, xg = ((x + xx * m[j:j + 1]).astype(BF16) for j in range(6))

        def dot(u, w):
            return jnp.dot(u.astype(BF16), w[...], preferred_element_type=F32)

        r_ref[...] = dot(xr, wb[0]).astype(r_ref.dtype)
        k_ref[...] = dot(xk, wb[1]).astype(k_ref.dtype)
        v_ref[...] = dot(xv, wb[2]).astype(v_ref.dtype)
        zw = w0_ref[...] + dot(jnp.tanh(dot(xw, wb[3])), wb[4])
        w_log = -(jnp.maximum(-zw, 0.0) + jnp.log(1.0 + jnp.exp(-jnp.abs(zw)))) - 0.5
        ld_ref[...] = -jnp.exp(w_log)
        a_ref[...] = _sigmoid(a0_ref[...] + dot(dot(xa, wb[5]), wb[6]))
        g_ref[...] = dot(_sigmoid(dot(xg, wb[7])), wb[8]).astype(g_ref.dtype)

    row = pl.BlockSpec((tm, D), lambda i: (i, 0))
    full = lambda a: pl.BlockSpec(a.shape, lambda i: (0, 0), pipeline_mode=pl.Buffered(1))
    vec = pl.BlockSpec((1, D), lambda i: (0, 0))
    return pl.pallas_call(
        kern,
        grid=(T // tm,),
        in_specs=[row,
                  pl.BlockSpec((halo, D), lambda i: (jnp.maximum(i * (tm // halo) - 1, 0), 0)),
                  pl.BlockSpec(mix.shape, lambda i: (0, 0)), vec, vec] + [full(w) for w in ws],
        out_specs=[row] * 6,
        out_shape=[jax.ShapeDtypeStruct((T, D), dt) for dt in (BF16, BF16, BF16, F32, F32, BF16)],
        scratch_shapes=[pltpu.VMEM(w.shape, BF16) for w in ws],
        compiler_params=_cparams(1),
        name="rwkv_in",
    )(xf, xf, mix, w0.reshape(1, D), a0.reshape(1, D), *ws)


def _rwkv_scan(r, k, v, ld, a, g, seq, k_k, k_a, r_k, gn_g, gn_b):
    T, D = r.shape
    C = RWKV_CHUNK
    N = RWKV_HEAD
    PW = 2 * N
    NP = D // PW
    NB = RWKV_SEQS_PER_STEP
    n_c = seq // C

    def kern(r_ref, k_ref, v_ref, ld_ref, a_ref, g_ref, kk_ref, ka_ref, rk_ref, gg_ref, gb_ref,
             o_ref, st_ref):
        @pl.when(pl.program_id(1) == 0)
        def _():
            st_ref[...] = jnp.zeros_like(st_ref)

        lane = lax.broadcasted_iota(jnp.int32, (C, PW), 1)
        rowi = lax.broadcasted_iota(jnp.int32, (C, PW), 0)
        lo = lane < N
        col = jnp.where(lo, lane, lane - N)
        strict = rowi > col
        incl = rowi >= col
        lo2 = lax.broadcasted_iota(jnp.int32, (2 * C, PW), 1) < N
        kr = lax.broadcasted_iota(jnp.int32, (PW, PW), 0) < N
        kc = lax.broadcasted_iota(jnp.int32, (PW, PW), 1) < N
        same_head = kr == kc
        ones_bd = same_head.astype(BF16)
        tri = (lax.broadcasted_iota(jnp.int32, (C, C), 0)
               >= lax.broadcasted_iota(jnp.int32, (C, C), 1)).astype(BF16)

        def pairs(x):
            return [x[:, p * PW:(p + 1) * PW] for p in range(NP)]

        def head_sum(x):
            xs = jnp.concatenate(pairs(x), axis=0).astype(BF16)
            s = jnp.dot(xs, ones_bd, preferred_element_type=F32)
            return jnp.concatenate([s[p * C:(p + 1) * C] for p in range(NP)], axis=1)

        at, bt, kt, rt, vp, gl, bonus = [], [], [], [], [], [], []
        for bb in range(NB):
            r_all, k_all, v_all = (t[bb].astype(F32) for t in (r_ref, k_ref, v_ref))
            a_all = a_ref[bb]
            ld_all = ld_ref[bb]
            ld_hi = ld_all.astype(BF16)
            ld_rest = ld_all - ld_hi.astype(F32)
            ld_mid = ld_rest.astype(BF16)
            ld_lo = (ld_rest - ld_mid.astype(F32)).astype(BF16)
            cum = (jnp.dot(tri, ld_hi, preferred_element_type=F32)
                   + jnp.dot(tri, ld_mid, preferred_element_type=F32)
                   + jnp.dot(tri, ld_lo, preferred_element_type=F32))
            kk = k_all * kk_ref[...]
            kk = kk * lax.rsqrt(jnp.maximum(head_sum(kk * kk), 1e-24))
            km = k_all * (1.0 + (a_all - 1.0) * ka_ref[...])
            gam = jnp.exp(cum)
            gam_inv = jnp.exp(-cum)
            at += pairs(-kk * jnp.exp(cum - ld_all))
            bt += pairs(kk * a_all * gam_inv)
            kt += pairs(km * gam_inv)
            rt += pairs(r_all * gam)
            vp += pairs(v_all)
            gl += pairs(gam[C - 1:C, :])
            bonus.append(head_sum(r_all * km * rk_ref[...]) * v_all)

        def nt(x, y):
            return lax.dot_general(x, y, (((1,), (1,)), ((), ())), preferred_element_type=F32)

        def split(x):
            z = jnp.zeros_like(x)
            return jnp.concatenate([jnp.where(lo, x, z), jnp.where(lo, z, x)], axis=0).astype(BF16)

        chains = range(NB * NP)
        ar = [jnp.concatenate([at[p], rt[p]], axis=0) for p in chains]
        ar_b = [x.astype(BF16) for x in ar]
        bk = [jnp.concatenate([bt[p], kt[p]], axis=0) for p in chains]
        kb_b = [jnp.concatenate([kt[p], bt[p]], axis=0).astype(BF16) for p in chains]
        st = [st_ref[p] for p in chains]
        m0 = [nt(jnp.where(lo2, ar[p], 0.0).astype(BF16), bk[p].astype(BF16)) for p in chains]
        m1 = [nt(jnp.where(lo2, 0.0, ar[p]).astype(BF16), kb_b[p]) for p in chains]
        ars = [nt(ar_b[p], st[p].astype(BF16)) for p in chains]
        lmat = [jnp.where(strict, jnp.where(lo, m0[p][:C], m1[p][:C]), 0.0).astype(BF16)
                for p in chains]
        akm = [jnp.where(strict, jnp.where(lo, m1[p][:C], m0[p][:C]), 0.0).astype(BF16)
               for p in chains]
        v_hi_lo = [jnp.concatenate([jnp.where(lo, 0.0, vp[p]), jnp.where(lo, vp[p], 0.0)],
                                   axis=0).astype(BF16) for p in chains]
        u = [ars[p][:C] + jnp.dot(akm[p], v_hi_lo[p], preferred_element_type=F32)
             for p in chains]
        n = 1
        while n < C:
            u = [u[p] + jnp.dot(lmat[p], split(u[p]), preferred_element_type=F32)
                 for p in chains]
            n *= 2
            if n < C:
                lmat = [jnp.dot(lmat[p], split(lmat[p]),
                                preferred_element_type=F32).astype(BF16) for p in chains]
        ys = []
        for p in chains:
            rmat = jnp.concatenate([jnp.where(incl, m0[p][C:], 0.0),
                                    jnp.where(incl, m1[p][C:], 0.0)], axis=1).astype(BF16)
            u0, u1 = jnp.where(lo, u[p], 0.0), jnp.where(lo, 0.0, u[p])
            v0, v1 = jnp.where(lo, vp[p], 0.0), jnp.where(lo, 0.0, vp[p])
            uvs = jnp.concatenate([u0, v0, v1, u1], axis=0).astype(BF16)
            ys.append(ars[p][C:] + jnp.dot(rmat, uvs, preferred_element_type=F32))
            uv = jnp.concatenate([u[p], vp[p]], axis=0).astype(BF16)
            upd = lax.dot_general(uv, (bk[p] * gl[p]).astype(BF16), (((0,), (0,)), ((), ())),
                                  preferred_element_type=F32)
            st_ref[p] = st[p] * gl[p] + jnp.where(same_head, upd, 0.0)

        for bb in range(NB):
            y = jnp.concatenate(ys[bb * NP:(bb + 1) * NP], axis=1)
            dy = y - head_sum(y) * (1.0 / N)
            var = head_sum(dy * dy) * (1.0 / N)
            yn = dy * lax.rsqrt(var + RWKV_GN_EPS) * gg_ref[...] + gb_ref[...]
            o_ref[bb] = ((yn + bonus[bb]) * g_ref[bb].astype(F32)).astype(o_ref.dtype)

    n_seq = T // seq
    row = pl.BlockSpec((NB, C, D), lambda bi, c: (bi, c, 0))
    vec = pl.BlockSpec((1, D), lambda bi, c: (0, 0))
    seqs = [t.reshape(n_seq, seq, D) for t in (r, k, v, ld, a, g)]
    out = pl.pallas_call(
        kern,
        grid=(n_seq // NB, n_c),
        in_specs=[row] * 6 + [vec] * 5,
        out_specs=row,
        out_shape=jax.ShapeDtypeStruct((n_seq, seq, D), BF16),
        scratch_shapes=[pltpu.VMEM((NB * NP, PW, PW), F32)],
        compiler_params=_cparams(2),
        name="rwkv_scan",
    )(*seqs, k_k.reshape(1, D), k_a.reshape(1, D), r_k.reshape(1, D),
      gn_g.reshape(1, D), gn_b.reshape(1, D))
    return out.reshape(T, D)


def _rwkv7_mixer(xf, seq, mix, w_r, w_k, w_v, w0, w1, w2, a0, a1, a2, g1, g2,
                 k_k, k_a, r_k, gn_g, gn_b, w_o, g, b):
    r, k, v, ld, a, gate = _rwkv_in(xf, seq, mix, w_r, w_k, w_v, w0, w1, w2, a0, a1, a2, g1, g2)
    y = _rwkv_scan(r, k, v, ld, a, gate, seq, k_k, k_a, r_k, gn_g, gn_b)
    return _mm_ln(y, w_o, (), xf, g, b, name="rwkv_out_ln")


def _rglru_core(proj, pos3, seq, conv_w, conv_b, w_ga, b_ga, w_gx, b_gx, lam):
    T = proj.shape[0]
    W = LRU_BW
    n_conv = conv_w.shape[0]

    def kern(gate_ref, u_ref, pos_ref, cw_ref, cb_ref, wga_ref, bga_ref, wgx_ref, bgx_ref,
             lam_ref, o_ref, sa_ref, sb_ref, sc_ref, pad_ref):
        u = u_ref[...]
        cw = cw_ref[...]
        uc = cw[n_conv - 1:n_conv] * u + cb_ref[...]
        pad_ref[0:F32_SUBLANES, :] = jnp.zeros((F32_SUBLANES, W), F32)
        pad_ref[F32_SUBLANES:, :] = u
        for d in range(1, n_conv):
            uc = uc + cw[n_conv - 1 - d:n_conv - d] * pad_ref[F32_SUBLANES - d:F32_SUBLANES - d + seq, :]
        ub = uc.astype(BF16)
        rg = _sigmoid(jnp.dot(ub, wga_ref[...].astype(BF16), preferred_element_type=F32)
                      + bga_ref[...])
        ig = _sigmoid(jnp.dot(ub, wgx_ref[...].astype(BF16), preferred_element_type=F32)
                      + bgx_ref[...])
        nl = -lam_ref[...]
        softplus = jnp.maximum(nl, 0.0) + jnp.log1p(jnp.exp(-jnp.abs(nl)))
        log_a = -LRU_C * rg * softplus
        reset = pos_ref[...] == 0
        a = jnp.where(reset, 0.0, jnp.exp(log_a))
        mult = jnp.where(reset, 1.0, jnp.sqrt(-_expm1(2.0 * log_a)))
        hb = _linear_scan_rows(a, mult * (ig * uc), sa_ref, sb_ref, sc_ref)
        gt = gate_ref[...]
        gelu = 0.5 * gt * (1.0 + jnp.tanh(math.sqrt(2.0 / math.pi) * (gt + 0.044715 * gt * gt * gt)))
        o_ref[...] = (gelu * hb).astype(o_ref.dtype)

    vec = pl.BlockSpec((1, W), lambda bi, j: (0, j))
    blk = pl.BlockSpec((None, W, W), lambda bi, j: (j, 0, 0))
    return pl.pallas_call(
        kern,
        grid=(T // seq, LRU_BLOCKS),
        in_specs=[pl.BlockSpec((seq, W), lambda bi, j: (bi, j)),
                  pl.BlockSpec((seq, W), lambda bi, j: (bi, LRU_BLOCKS + j)),
                  pl.BlockSpec((None, seq, 1), lambda bi, j: (bi, 0, 0)),
                  pl.BlockSpec((n_conv, W), lambda bi, j: (0, j)),
                  vec, blk, vec, blk, vec, vec],
        out_specs=pl.BlockSpec((seq, W), lambda bi, j: (bi, j)),
        out_shape=jax.ShapeDtypeStruct((T, D_RNN), BF16),
        scratch_shapes=[pltpu.VMEM((seq, W), F32)] * 3 + [pltpu.VMEM((seq + F32_SUBLANES, W), F32)],
        compiler_params=_cparams(2),
        name="rglru_core",
    )(proj, proj, pos3, conv_w, conv_b.reshape(1, -1), w_ga, b_ga.reshape(1, -1),
      w_gx, b_gx.reshape(1, -1), lam.reshape(1, -1))


def _rglru_mixer(xf, xb, pos3, seq, w_in, conv_w, conv_b, w_ga, b_ga, w_gx, b_gx, lam, w_out, g, b):
    (proj,) = _mm(xb, [(w_in, (), 0)], _epi_plain, [(2 * D_RNN, F32)], tm=2048, tn=512,
                  name="rglru_in")
    y = _rglru_core(proj, pos3, seq, conv_w, conv_b, w_ga, b_ga, w_gx, b_gx, lam)
    return _mm_ln(y, w_out, (), xf, g, b, name="rglru_out_ln")


def _rope_tables(pos_col, freq, *, tm=2048):
    T = pos_col.shape[0]
    half = freq.shape[1]

    def kern(pos_ref, f_ref, cos_ref, sin_ref):
        ang = pos_ref[...].astype(F32) * f_ref[...]
        cos_ref[...] = jnp.cos(ang)
        sin_ref[...] = jnp.sin(ang)

    out = pl.BlockSpec((tm, half), lambda i: (i, 0))
    return pl.pallas_call(
        kern,
        grid=(T // tm,),
        in_specs=[pl.BlockSpec((tm, 1), lambda i: (i, 0)), pl.BlockSpec((1, half), lambda i: (0, 0))],
        out_specs=[out, out],
        out_shape=[jax.ShapeDtypeStruct((T, half), F32)] * 2,
        compiler_params=_cparams(1),
        name="rope_tables",
    )(pos_col, freq)


def _retention_core(qk, vg, seq):
    T = qk.shape[0]
    L, CH = RET_SUPER, RET_CHUNK
    H, DK, DV = RET_HEADS, RET_DK, RET_DV
    n_l = seq // L

    def kern(qk_ref, vg_ref, o_ref, st_ref, decay_ref, qdec_ref, kdec_ref):
        @pl.when(jnp.logical_and(pl.program_id(0) == 0, pl.program_id(1) == 0))
        def _():
            ri = lax.broadcasted_iota(jnp.int32, (L, L), 0)
            ci = lax.broadcasted_iota(jnp.int32, (L, L), 1)
            dist = jnp.abs(ri - ci).astype(F32)
            allowed = ci // CH <= ri // CH
            idx = lax.broadcasted_iota(jnp.int32, (L, 1), 0).astype(F32)
            for h in range(H):
                log_g = math.log1p(-(2.0 ** (-5.0 - h)))
                decay_ref[h] = jnp.where(allowed, jnp.exp(dist * log_g), 0.0)
                qdec_ref[h] = jnp.exp((idx + 1.0) * log_g)
                kdec_ref[h] = jnp.exp((L - 1.0 - idx) * log_g)

        @pl.when(pl.program_id(1) == 0)
        def _():
            st_ref[...] = jnp.zeros_like(st_ref)

        for h in range(H):
            q = qk_ref[:, h * DK:(h + 1) * DK]
            k = qk_ref[:, (H + h) * DK:(H + h + 1) * DK]
            v = vg_ref[:, h * DV:(h + 1) * DV]
            g = vg_ref[:, (H + h) * DV:(H + h + 1) * DV]
            st = st_ref[h]
            scores = lax.dot_general(q, k, (((1,), (1,)), ((), ())), preferred_element_type=F32)
            scores = (scores * decay_ref[h]).astype(BF16)
            qd = (q.astype(F32) * qdec_ref[h]).astype(BF16)
            y = (jnp.dot(scores, v, preferred_element_type=F32)
                 + jnp.dot(qd, st.astype(BF16), preferred_element_type=F32))
            kd = (k.astype(F32) * kdec_ref[h]).astype(BF16)
            c_dec = math.exp(L * math.log1p(-(2.0 ** (-5.0 - h))))
            st_ref[h] = st * c_dec + lax.dot_general(kd, v, (((0,), (0,)), ((), ())),
                                                     preferred_element_type=F32)
            yn = y * lax.rsqrt(jnp.mean(y * y, axis=-1, keepdims=True) + RET_GN_EPS)
            o_ref[:, h * DV:(h + 1) * DV] = (_silu(g.astype(F32)) * yn).astype(o_ref.dtype)

    return pl.pallas_call(
        kern,
        grid=(T // seq, n_l),
        in_specs=[pl.BlockSpec((L, 2 * H * DK), lambda bi, l: (bi * n_l + l, 0)),
                  pl.BlockSpec((L, 2 * H * DV), lambda bi, l: (bi * n_l + l, 0))],
        out_specs=pl.BlockSpec((L, H * DV), lambda bi, l: (bi * n_l + l, 0)),
        out_shape=jax.ShapeDtypeStruct((T, H * DV), BF16),
        scratch_shapes=[pltpu.VMEM((H, DK, DV), F32), pltpu.VMEM((H, L, L), F32),
                        pltpu.VMEM((H, L, 1), F32), pltpu.VMEM((H, L, 1), F32)],
        compiler_params=_cparams(2),
        name="retention_core",
    )(qk, vg)


def _retention_mixer(xf, xb, pos_col, seq, w_in, w_o, g, b):
    D = xf.shape[1]
    half = RET_DK // 2
    freq = (ROPE_BASE ** -jnp.linspace(0.0, 1.0, half, dtype=F32)).reshape(1, half)
    cos, sin = _rope_tables(pos_col, freq)
    n_q = D // RET_DK

    def epi_rope(accs, rows, cols, j):
        t1, t2 = accs[0][:, :half], accs[0][:, half:]
        c, s = rows
        scale = jnp.where(j >= n_q, RET_DK ** -0.5, 1.0).astype(F32)
        return (jnp.concatenate([t1 * c - t2 * s, t1 * s + t2 * c], axis=1) * scale,)

    (qk,) = _mm(xb, [(w_in, (), 0)], epi_rope, [(2 * D, BF16)], tm=2048, tn=RET_DK,
                name="ret_qk_rope", rows=[(cos, half, lambda j: 0), (sin, half, lambda j: 0)])
    (vg,) = _mm(xb, [(w_in, (), 2 * D // 512)], _epi_plain, [(4 * D, BF16)], tm=2048, tn=512,
                name="ret_vg")
    y = _retention_core(qk, vg, seq)
    return _mm_ln(y, w_o, (), xf, g, b, name="ret_out_ln")


def _moe_router(xf, w_router, *, tm=1024):
    T, D = xf.shape
    E = w_router.shape[1]
    LANES = 128
    wpad = jnp.pad(w_router, ((0, 0), (0, LANES - E)))

    def kern(x_ref, w_ref, id_ref, p_ref):
        x = x_ref[...]
        w = w_ref[...]
        x_hi = x.astype(BF16)
        x_lo = (x - x_hi.astype(F32)).astype(BF16)
        w_hi = w.astype(BF16)
        w_lo = (w - w_hi.astype(F32)).astype(BF16)
        logits = (jnp.dot(x_hi, w_hi, preferred_element_type=F32)
                  + jnp.dot(x_lo, w_hi, preferred_element_type=F32)
                  + jnp.dot(x_hi, w_lo, preferred_element_type=F32))
        lane = lax.broadcasted_iota(jnp.int32, logits.shape, 1).astype(F32)
        neg = jnp.float32(-jnp.inf)
        l1 = jnp.where(lane < E, logits, neg)
        m1 = jnp.max(l1, axis=-1, keepdims=True)
        i1 = jnp.min(jnp.where(l1 == m1, lane, float(LANES)), axis=-1, keepdims=True)
        l2 = jnp.where(lane == i1, neg, l1)
        m2 = jnp.max(l2, axis=-1, keepdims=True)
        i2 = jnp.min(jnp.where(l2 == m2, lane, float(LANES)), axis=-1, keepdims=True)
        e = jnp.exp(m2 - m1)
        p1 = 1.0 / (1.0 + e)
        p2 = e / (1.0 + e)
        id_ref[...] = jnp.where(lane == 0, i1, jnp.where(lane == 1, i2, 0.0)).astype(jnp.int32)
        p_ref[...] = jnp.where(lane == 0, p1, jnp.where(lane == 1, p2, 0.0))

    out = pl.BlockSpec((tm, LANES), lambda i: (i, 0))
    return pl.pallas_call(
        kern,
        grid=(T // tm,),
        in_specs=[pl.BlockSpec((tm, D), lambda i: (i, 0)), pl.BlockSpec((D, LANES), lambda i: (0, 0))],
        out_specs=[out, out],
        out_shape=[jax.ShapeDtypeStruct((T, LANES), jnp.int32), jax.ShapeDtypeStruct((T, LANES), F32)],
        compiler_params=_cparams(1),
        name="moe_router",
    )(xf, wpad)


def _moe_plan(ids, n_tiles):
    T = ids.shape[0]
    e_flat = ids.reshape(-1)
    onehot = (e_flat[:, None] == jnp.arange(N_EXPERTS, dtype=jnp.int32)[None, :]).astype(jnp.int32)
    csum = jnp.cumsum(onehot, axis=0)
    rank = jnp.sum((csum - onehot) * onehot, axis=1)
    counts = csum[-1]
    padded = ((counts + MOE_TILE - 1) // MOE_TILE) * MOE_TILE
    ends = jnp.cumsum(padded)
    starts = ends - padded
    pos = (jnp.sum(onehot * starts[None, :], axis=1) + rank).astype(jnp.int32)
    tile_start = jnp.arange(n_tiles, dtype=jnp.int32) * MOE_TILE
    tile_e = jnp.sum((tile_start[:, None] >= ends[None, :]).astype(jnp.int32), axis=1)
    tile_e = jnp.minimum(tile_e, N_EXPERTS - 1).astype(jnp.int32)
    n_used = (ends[-1] // MOE_TILE).astype(jnp.int32).reshape(1)
    last_tile = jnp.where(padded > 0, ends // MOE_TILE - 1, -1)
    tail = n_tiles - N_EXPERTS + jnp.arange(N_EXPERTS, dtype=jnp.int32)
    tail = jnp.where(tail >= n_used[0], tail, -1)
    fill_tiles = jnp.concatenate([last_tile, tail]).astype(jnp.int32)
    return pos.reshape(T, TOP_K), tile_e, n_used, fill_tiles


def _moe_dispatch(xf, pos, fill_tiles, n_rows, *, tq=1024):
    T, D = xf.shape
    n_steps = T // tq
    pos_blocks = pos.reshape(n_steps, 1, tq * TOP_K)
    n_fill = fill_tiles.shape[0]

    def kern(fill_ref, pos_ref, x_ref, xs_ref, zero_ref, sem, zsem):
        @pl.when(pl.program_id(0) == 0)
        def _():
            zero_ref[...] = jnp.zeros_like(zero_ref)

            def fill(z):
                row0 = pl.multiple_of(fill_ref[z] * MOE_TILE, MOE_TILE)
                return pltpu.make_async_copy(zero_ref, xs_ref.at[pl.ds(row0, MOE_TILE)], zsem)

            for z in range(n_fill):
                @pl.when(fill_ref[z] >= 0)
                def _():
                    fill(z).start()
            for z in range(n_fill):
                @pl.when(fill_ref[z] >= 0)
                def _():
                    fill(z).wait()

        def copy(r, k):
            dst = pos_ref[0, TOP_K * r + k]
            return pltpu.make_async_copy(x_ref.at[pl.ds(r, 1)], xs_ref.at[pl.ds(dst, 1)], sem)

        def start(r, c):
            for k in range(TOP_K):
                copy(r, k).start(priority=k % 2)
            return c

        def wait(r, c):
            for k in range(TOP_K):
                copy(r, k).wait()
            return c

        lax.fori_loop(0, tq, start, 0, unroll=DMA_UNROLL)
        lax.fori_loop(0, tq, wait, 0, unroll=DMA_UNROLL)

    return pl.pallas_call(
        kern,
        grid_spec=pltpu.PrefetchScalarGridSpec(
            num_scalar_prefetch=1,
            grid=(n_steps,),
            in_specs=[pl.BlockSpec((None, 1, tq * TOP_K), lambda i, ft: (i, 0, 0),
                                   memory_space=pltpu.SMEM),
                      pl.BlockSpec((tq, D), lambda i, ft: (i, 0))],
            out_specs=pl.BlockSpec(memory_space=pl.ANY),
            scratch_shapes=[pltpu.VMEM((MOE_TILE, D), F32), pltpu.SemaphoreType.DMA,
                            pltpu.SemaphoreType.DMA]),
        out_shape=jax.ShapeDtypeStruct((n_rows, D), F32),
        compiler_params=_cparams(1),
        name="moe_dispatch",
    )(fill_tiles, pos_blocks, xf)


def _moe_experts(xs, tile_e, n_used, w_gu, w_down, layer):
    P, D = xs.shape
    F = w_down.shape[2]
    tm = MOE_TILE
    n_tiles = P // tm
    tn_up = 1792
    n_f = F // tn_up
    tn_dn = 1024

    def tile(i, nu):
        return jnp.minimum(i, nu[0] - 1)

    def fresh(te, nu):
        i = pl.program_id(1)
        prev = te[jnp.maximum(i - 1, 0)]
        return jnp.logical_and(i < nu[0], jnp.logical_or(i == 0, te[i] != prev))

    def up_kern(te, nu, x_ref, wg_ref, wu_ref, h_ref, wgb, wub):
        @pl.when(fresh(te, nu))
        def _():
            wgb[...] = wg_ref[...].astype(BF16)
            wub[...] = wu_ref[...].astype(BF16)

        @pl.when(pl.program_id(1) < nu[0])
        def _():
            xv = x_ref[...].astype(BF16)
            gt = jnp.dot(xv, wgb[...], preferred_element_type=F32)
            up = jnp.dot(xv, wub[...], preferred_element_type=F32)
            h_ref[...] = (_silu(gt) * up).astype(h_ref.dtype)

        @pl.when(pl.program_id(1) >= nu[0])
        def _():
            h_ref[...] = jnp.zeros_like(h_ref)

    h = pl.pallas_call(
        up_kern,
        grid_spec=pltpu.PrefetchScalarGridSpec(
            num_scalar_prefetch=2,
            grid=(n_f, n_tiles),
            in_specs=[pl.BlockSpec((tm, D), lambda f, i, te, nu: (tile(i, nu), 0)),
                      pl.BlockSpec((None, None, D, tn_up),
                                   lambda f, i, te, nu: (layer, te[tile(i, nu)], 0, f)),
                      pl.BlockSpec((None, None, D, tn_up),
                                   lambda f, i, te, nu: (layer, te[tile(i, nu)], 0, n_f + f))],
            out_specs=pl.BlockSpec((tm, tn_up), lambda f, i, te, nu: (i, f)),
            scratch_shapes=[pltpu.VMEM((D, tn_up), BF16), pltpu.VMEM((D, tn_up), BF16)]),
        out_shape=jax.ShapeDtypeStruct((P, F), BF16),
        compiler_params=_cparams(2),
        name="moe_up",
    )(tile_e, n_used, xs, w_gu, w_gu)

    def down_kern(te, nu, h_ref, wd_ref, y_ref, wdb):
        @pl.when(fresh(te, nu))
        def _():
            wdb[...] = wd_ref[...].astype(BF16)

        @pl.when(pl.program_id(1) < nu[0])
        def _():
            y_ref[...] = jnp.dot(h_ref[...], wdb[...], preferred_element_type=F32)

        @pl.when(pl.program_id(1) >= nu[0])
        def _():
            y_ref[...] = jnp.zeros_like(y_ref)

    return pl.pallas_call(
        down_kern,
        grid_spec=pltpu.PrefetchScalarGridSpec(
            num_scalar_prefetch=2,
            grid=(D // tn_dn, n_tiles),
            in_specs=[pl.BlockSpec((tm, F), lambda n, i, te, nu: (tile(i, nu), 0)),
                      pl.BlockSpec((None, None, F, tn_dn),
                                   lambda n, i, te, nu: (layer, te[tile(i, nu)], 0, n))],
            out_specs=pl.BlockSpec((tm, tn_dn), lambda n, i, te, nu: (i, n)),
            scratch_shapes=[pltpu.VMEM((F, tn_dn), BF16)]),
        out_shape=jax.ShapeDtypeStruct((P, D), F32),
        compiler_params=_cparams(2),
        name="moe_down",
    )(tile_e, n_used, h, w_down)


def _moe_combine_ln(xf, ys, pos, probs, g, b, *, tq=1024):
    T, D = xf.shape
    n_steps = T // tq
    pos_blocks = pos.reshape(n_steps, 1, tq * TOP_K)

    def kern(pos_ref, x_ref, p_ref, g_ref, b_ref, ys_ref, of_ref, ob_ref, buf, sem):
        def copy(r, k):
            src = pos_ref[0, TOP_K * r + k]
            return pltpu.make_async_copy(ys_ref.at[pl.ds(src, 1)], buf.at[k, pl.ds(r, 1)], sem)

        def start(r, c):
            for k in range(TOP_K):
                copy(r, k).start(priority=k % 2)
            return c

        def wait(r, c):
            for k in range(TOP_K):
                copy(r, k).wait()
            return c

        lax.fori_loop(0, tq, start, 0, unroll=DMA_UNROLL)
        lax.fori_loop(0, tq, wait, 0, unroll=DMA_UNROLL)
        p = p_ref[...]
        z = ALPHA * x_ref[...] + p[:, 0:1] * buf[0] + p[:, 1:2] * buf[1]
        mu = jnp.mean(z, axis=-1, keepdims=True)
        d = z - mu
        var = jnp.mean(d * d, axis=-1, keepdims=True)
        y = d * lax.rsqrt(var + LN_EPS) * g_ref[...] + b_ref[...]
        of_ref[...] = y
        ob_ref[...] = y.astype(ob_ref.dtype)

    row = pl.BlockSpec((tq, D), lambda i: (i, 0))
    vec = pl.BlockSpec((1, D), lambda i: (0, 0))
    return pl.pallas_call(
        kern,
        grid=(n_steps,),
        in_specs=[pl.BlockSpec((None, 1, tq * TOP_K), lambda i: (i, 0, 0), memory_space=pltpu.SMEM),
                  row, pl.BlockSpec((tq, probs.shape[1]), lambda i: (i, 0)), vec, vec,
                  pl.BlockSpec(memory_space=pl.ANY)],
        out_specs=[row, row],
        out_shape=[jax.ShapeDtypeStruct((T, D), F32), jax.ShapeDtypeStruct((T, D), BF16)],
        scratch_shapes=[pltpu.VMEM((TOP_K, tq, D), F32), pltpu.SemaphoreType.DMA],
        compiler_params=_cparams(1),
        name="moe_combine_ln",
    )(pos_blocks, xf, probs, g.reshape(1, D), b.reshape(1, D), ys)


def _moe_ffn(xf, w_router, w_gu, w_down, layer, g, b):
    T = xf.shape[0]
    n_tiles = T * TOP_K // MOE_TILE + N_EXPERTS
    ids, probs = _moe_router(xf, w_router[layer])
    pos, tile_e, n_used, fill_tiles = _moe_plan(ids[:, :TOP_K], n_tiles)
    xs = _moe_dispatch(xf, pos, fill_tiles, n_tiles * MOE_TILE)
    ys = _moe_experts(xs, tile_e, n_used, w_gu, w_down, layer)
    return _moe_combine_ln(xf, ys, pos, probs, g, b)


def kernel(x, positions, ln_mix_g, ln_mix_b, ln_ffn_g, ln_ffn_b, a_w_in, a_conv_w, a_w_out, b_mix, b_w_r, b_w_k, b_w_v, b_w0, b_w1, b_w2, b_a0, b_a1, b_a2, b_g1, b_g2, b_k_k, b_k_a, b_r_k, b_gn_g, b_gn_b, b_w_o, c_w_in, c_conv_w, c_conv_b, c_w_ga, c_b_ga, c_w_gx, c_b_gx, c_lam, c_w_out, d_w_in, d_w_o, ffn_w_gu, ffn_w_down, moe_w_router, moe_w_gu, moe_w_down):
    bsz, seq, D = x.shape
    T = bsz * seq
    xf = x.reshape(T, D)
    pos3 = positions.reshape(bsz, seq, 1)
    pos_col = positions.reshape(T, 1)

    xf, xb = _short_conv_mixer(xf, xf, seq, a_w_in, a_conv_w, a_w_out, ln_mix_g[0], ln_mix_b[0])
    xf, xb = _dense_ffn(xf, xb, ffn_w_gu, ffn_w_down, 0, ln_ffn_g[0], ln_ffn_b[0])

    xf, xb = _rwkv7_mixer(xf, seq, b_mix, b_w_r, b_w_k, b_w_v, b_w0, b_w1, b_w2, b_a0, b_a1, b_a2,
                          b_g1, b_g2, b_k_k, b_k_a, b_r_k, b_gn_g, b_gn_b, b_w_o,
                          ln_mix_g[1], ln_mix_b[1])
    xf, xb = _moe_ffn(xf, moe_w_router, moe_w_gu, moe_w_down, 0, ln_ffn_g[1], ln_ffn_b[1])

    xf, xb = _rglru_mixer(xf, xb, pos3, seq, c_w_in, c_conv_w, c_conv_b, c_w_ga, c_b_ga, c_w_gx,
                          c_b_gx, c_lam, c_w_out, ln_mix_g[2], ln_mix_b[2])
    xf, xb = _dense_ffn(xf, xb, ffn_w_gu, ffn_w_down, 1, ln_ffn_g[2], ln_ffn_b[2])

    xf, xb = _retention_mixer(xf, xb, pos_col, seq, d_w_in, d_w_o, ln_mix_g[3], ln_mix_b[3])
    xf, xb = _moe_ffn(xf, moe_w_router, moe_w_gu, moe_w_down, 1, ln_ffn_g[3], ln_ffn_b[3])
    return xf.reshape(bsz, seq, D)
```
